```python
import math
import jax, jax.numpy as jnp
from jax import lax
import numpy as np

D_MODEL = 1024
BATCH = 4
SEQ = 8192
DEPTH = 1

N_HEADS = 8
HEAD_DIM = 128
ATTN_WIDTH = N_HEADS * HEAD_DIM
CONV_WIDTH = D_MODEL
CONV_K = 3
D_FF = 2816
PLE_DIM = 256
Q_BLOCK = 128
NORM_EPS = 1e-6
MIX_COLS = 3 * CONV_WIDTH + 3 * ATTN_WIDTH + 2 * D_MODEL

kernel_name = "hybrid_shortconv_stickbreaking_macaron_block"


def rms_norm(x, g):
    xf = x.astype(jnp.float32)
    y = xf * lax.rsqrt(jnp.mean(xf * xf, axis=-1, keepdims=True) + NORM_EPS)
    return (y * g.astype(jnp.float32)).astype(x.dtype)


def swiglu(x, w_in, w_out):
    gate, up = jnp.split(x @ w_in, 2, axis=-1)
    return (jax.nn.silu(gate) * up) @ w_out


def causal_depthwise_conv(x, w):
    return lax.conv_general_dilated(
        x, w[:, None, :].astype(x.dtype), window_strides=(1,),
        padding=[(CONV_K - 1, 0)], dimension_numbers=('NWC', 'WIO', 'NWC'),
        feature_group_count=x.shape[-1])


def stick_breaking_attention(q, k, v):
    b, h, s, d = q.shape
    nblk = s // Q_BLOCK
    scale = 1.0 / math.sqrt(d)
    k_pos = jnp.arange(s)
    vf = v.astype(jnp.float32)
    qb = q.reshape(b, h, nblk, Q_BLOCK, d).transpose(2, 0, 1, 3, 4)

    def block(args):
        q_blk, blk_idx = args
        q_pos = blk_idx * Q_BLOCK + jnp.arange(Q_BLOCK)
        z = jnp.einsum('bhqd,bhkd->bhqk', q_blk, k,
                       preferred_element_type=jnp.float32) * scale
        mask = k_pos[None, :] < q_pos[:, None]
        log_1m_beta = jnp.where(mask, jax.nn.log_sigmoid(-z), 0.0)
        tail = lax.cumsum(log_1m_beta, axis=3, reverse=True) - log_1m_beta
        a = jnp.where(mask, jnp.exp(jax.nn.log_sigmoid(z) + tail), 0.0)
        return jnp.einsum('bhqk,bhkd->bhqd', a, vf).astype(q.dtype)

    out = lax.map(block, (qb, jnp.arange(nblk)))
    return out.transpose(1, 2, 0, 3, 4).reshape(b, h, s, d)


def mix_split_points():
    widths = [CONV_WIDTH, CONV_WIDTH, CONV_WIDTH, ATTN_WIDTH, ATTN_WIDTH, ATTN_WIDTH, D_MODEL, D_MODEL]
    pts, acc = [], 0
    for w in widths[:-1]:
        acc += w
        pts.append(acc)
    return pts


def setup_inputs(seed: int = 0) -> dict:
    key = jax.random.key(seed)
    ks = jax.random.split(key, 20)

    def w(k, shape, fan_in):
        return jax.random.normal(k, shape, jnp.float32) * (fan_in ** -0.5)

    def gain(k, shape):
        return 1.0 + 0.01 * jax.random.normal(k, shape, jnp.float32)

    return {
        "x": jax.random.normal(ks[0], (BATCH, SEQ, D_MODEL), jnp.float32),
        "p": jax.random.normal(ks[1], (DEPTH, BATCH, SEQ, PLE_DIM), jnp.float32),
        "ffn1_norm": gain(ks[2], (DEPTH, D_MODEL)),
        "ffn1_w_in": w(ks[3], (DEPTH, D_MODEL, 2 * D_FF), D_MODEL),
        "ffn1_w_out": w(ks[4], (DEPTH, D_FF, D_MODEL), D_FF),
        "mix_norm": gain(ks[5], (DEPTH, D_MODEL)),
        "w_mix_in": w(ks[6], (DEPTH, D_MODEL, MIX_COLS), D_MODEL),
        "conv_w": w(ks[7], (DEPTH, CONV_K, CONV_WIDTH), CONV_K),
        "w_conv_out": w(ks[8], (DEPTH, CONV_WIDTH, D_MODEL), CONV_WIDTH),
        "w_attn_out": w(ks[9], (DEPTH, ATTN_WIDTH, D_MODEL), ATTN_WIDTH),
        "w_mix_out": w(ks[10], (DEPTH, D_MODEL, D_MODEL), D_MODEL),
        "ffn2_norm": gain(ks[11], (DEPTH, D_MODEL)),
        "ffn2_w_in": w(ks[12], (DEPTH, D_MODEL, 2 * D_FF), D_MODEL),
        "ffn2_w_out": w(ks[13], (DEPTH, D_FF, D_MODEL), D_FF),
        "ple_norm": gain(ks[14], (DEPTH, D_MODEL)),
        "w_ple_gate": w(ks[15], (DEPTH, D_MODEL, D_MODEL), D_MODEL),
        "w_ple_proj": w(ks[16], (DEPTH, PLE_DIM, D_MODEL), PLE_DIM),
        "final_norm": gain(ks[17], (D_MODEL,)),
    }


def reference(x, p, ffn1_norm, ffn1_w_in, ffn1_w_out, mix_norm, w_mix_in, conv_w,
              w_conv_out, w_attn_out, w_mix_out, ffn2_norm, ffn2_w_in, ffn2_w_out,
              ple_norm, w_ple_gate, w_ple_proj, final_norm):
    b, s, _ = x.shape
    splits = mix_split_points()
    h = x
    for i in range(DEPTH):
        h = h + 0.5 * swiglu(rms_norm(h, ffn1_norm[i]), ffn1_w_in[i], ffn1_w_out[i])

        u = rms_norm(h, mix_norm[i])
        c_b, c_c, c_x, q, k, v, g_conv, g_attn = jnp.split(u @ w_mix_in[i], splits, axis=-1)

        y_conv = (c_b * causal_depthwise_conv(c_c * c_x, conv_w[i])) @ w_conv_out[i]

        def heads(t):
            return t.reshape(b, s, N_HEADS, HEAD_DIM).transpose(0, 2, 1, 3)
        o = stick_breaking_attention(heads(q), heads(k), heads(v))
        y_attn = o.transpose(0, 2, 1, 3).reshape(b, s, ATTN_WIDTH) @ w_attn_out[i]

        merged = jax.nn.sigmoid(g_conv) * y_conv + jax.nn.sigmoid(g_attn) * y_attn
        h = h + merged @ w_mix_out[i]

        h = h + 0.5 * swiglu(rms_norm(h, ffn2_norm[i]), ffn2_w_in[i], ffn2_w_out[i])

        ple_gate = jax.nn.sigmoid(rms_norm(h, ple_norm[i]) @ w_ple_gate[i])
        h = h + ple_gate * (p[i] @ w_ple_proj[i])

    return rms_norm(h, final_norm)
```

```python
import functools
import math

import jax
import jax.numpy as jnp
from jax import lax
from jax.experimental import pallas as pl
from jax.experimental.pallas import tpu as pltpu

NORM_EPS = 1e-6
N_HEADS = 8
HEAD_DIM = 128
CONV_K = 3

V7X_VMEM_LIMIT_BYTES = 56 * 1024 * 1024
SUBLANES = 8

FFN_ROWS = 512
MIX_ROWS = 512
POST_ROWS = 256
ATTN_BLOCK = 256

TAIL_EXIT = -104.0

_dot = functools.partial(jnp.dot, preferred_element_type=jnp.float32)
BF16 = jnp.bfloat16
F32 = jnp.float32


def _rms_norm(x, g):
    ms = jnp.mean(x * x, axis=-1, keepdims=True)
    return x * lax.rsqrt(ms + NORM_EPS) * g


def _swiglu(xn, w_in_ref, w_out_ref):
    d_ff = w_out_ref.shape[0]
    gate = _dot(xn, w_in_ref[:, 0:d_ff])
    up = _dot(xn, w_in_ref[:, d_ff:2 * d_ff])
    act = (gate * jax.nn.sigmoid(gate) * up).astype(BF16)
    return _dot(act, w_out_ref[...])


def _resident(shape):
    return pl.BlockSpec(shape, lambda *_: (0,) * len(shape), pipeline_mode=pl.Buffered(1))


def _params(n_axes):
    return pltpu.CompilerParams(
        dimension_semantics=("arbitrary",) * n_axes,
        vmem_limit_bytes=V7X_VMEM_LIMIT_BYTES,
    )


def _ffn_kernel(x_ref, g_ref, w_in_ref, w_out_ref, o_ref):
    x = x_ref[...]
    xn = _rms_norm(x, g_ref[...]).astype(BF16)
    o_ref[...] = x + 0.5 * _swiglu(xn, w_in_ref, w_out_ref)


def _ffn(x, g, w_in, w_out):
    t, d = x.shape
    rows = pl.BlockSpec((FFN_ROWS, d), lambda i: (i, 0))
    return pl.pallas_call(
        _ffn_kernel,
        out_shape=jax.ShapeDtypeStruct((t, d), F32),
        grid=(t // FFN_ROWS,),
        in_specs=[rows, _resident(g.shape), _resident(w_in.shape), _resident(w_out.shape)],
        out_specs=rows,
        compiler_params=_params(1),
        name="ffn",
    )(x, g, w_in, w_out)


def _mix_kernel(h_ref, g_ref, w_ref, cw_ref, wco_ref,
                q_ref, k_ref, v_ref, sg_ref, mc_ref, cc_buf, *, tiles_per_seq):
    tm, d = h_ref.shape
    u = _rms_norm(h_ref[...], g_ref[...]).astype(BF16)

    def proj(j):
        return _dot(u, w_ref[:, j * d:(j + 1) * d])

    @pl.when(pl.program_id(0) % tiles_per_seq == 0)
    def _():
        cc_buf[0:SUBLANES, :] = jnp.zeros((SUBLANES, d), F32)

    cc = proj(1) * proj(2)
    cc_buf[SUBLANES:SUBLANES + tm, :] = cc
    conv = (cw_ref[0:1, :] * cc_buf[SUBLANES - 2:SUBLANES - 2 + tm, :]
            + cw_ref[1:2, :] * cc_buf[SUBLANES - 1:SUBLANES - 1 + tm, :]
            + cw_ref[2:3, :] * cc)
    cc_buf[0:SUBLANES, :] = cc_buf[tm:tm + SUBLANES, :]

    y_conv = _dot((proj(0) * conv).astype(BF16), wco_ref[...])
    mc_ref[...] = (jax.nn.sigmoid(proj(6)) * y_conv).astype(BF16)
    q_ref[...] = proj(3).astype(BF16)
    k_ref[...] = proj(4).astype(BF16)
    v_ref[...] = proj(5).astype(BF16)
    sg_ref[...] = jax.nn.sigmoid(proj(7)).astype(BF16)


def _mix(h, g, w_mix_in, conv_w, w_conv_out, seq):
    t, d = h.shape
    rows = pl.BlockSpec((MIX_ROWS, d), lambda i: (i, 0))
    out = jax.ShapeDtypeStruct((t, d), BF16)
    return pl.pallas_call(
        functools.partial(_mix_kernel, tiles_per_seq=seq // MIX_ROWS),
        out_shape=(out,) * 5,
        grid=(t // MIX_ROWS,),
        in_specs=[rows, _resident(g.shape), _resident(w_mix_in.shape), _resident(conv_w.shape),
                  _resident(w_conv_out.shape)],
        out_specs=(rows,) * 5,
        scratch_shapes=[pltpu.VMEM((MIX_ROWS + SUBLANES, d), F32)],
        compiler_params=_params(1),
        name="mix",
    )(h, g, w_mix_in, conv_w, w_conv_out)


def _attn_kernel(q_ref, k_ref, v_ref, tri_ref, o_ref, acc_ref, carry_ref, *, scale):
    blk = q_ref.shape[0]
    i = pl.program_id(2)
    q = q_ref[...]
    tri = tri_ref[...]

    def log_sigmoids(k_blk):
        z = lax.dot_general(q, k_blk, (((1,), (1,)), ((), ())), preferred_element_type=F32) * scale
        softplus = jnp.maximum(z, 0.0) + jnp.log(1.0 + jnp.exp(-jnp.abs(z)))
        return -softplus, z - softplus

    def tail_sum(l):
        hi = l.astype(BF16)
        lo = (l - hi.astype(F32)).astype(BF16)
        return _dot(hi, tri) + _dot(lo, tri)

    start = pl.multiple_of(i * blk, blk)
    log_1m_beta, log_beta = log_sigmoids(k_ref[pl.ds(start, blk), :])
    causal = (lax.broadcasted_iota(jnp.int32, (blk, blk), 1)
              < lax.broadcasted_iota(jnp.int32, (blk, blk), 0))
    l = jnp.where(causal, log_1m_beta, 0.0)
    a = jnp.where(causal, jnp.exp(log_beta + tail_sum(l)), 0.0)
    acc_ref[...] = _dot(a.astype(BF16), v_ref[pl.ds(start, blk), :])
    carry = jnp.sum(l, axis=-1, keepdims=True)
    carry_ref[...] = carry

    def more(state):
        j, max_carry = state
        return jnp.logical_and(j >= 0, max_carry > TAIL_EXIT)

    def step(state):
        j, _ = state
        start = pl.multiple_of(j * blk, blk)
        log_1m_beta, log_beta = log_sigmoids(k_ref[pl.ds(start, blk), :])
        carry = carry_ref[...]
        a = jnp.exp(log_beta + tail_sum(log_1m_beta) + carry)
        acc_ref[...] += _dot(a.astype(BF16), v_ref[pl.ds(start, blk), :])
        carry = carry + jnp.sum(log_1m_beta, axis=-1, keepdims=True)
        carry_ref[...] = carry
        return j - 1, jnp.max(carry)

    lax.while_loop(more, step, (i - 1, jnp.max(carry)))
    o_ref[...] = acc_ref[...].astype(o_ref.dtype)


def _attention(q, k, v):
    b, s, _ = q.shape
    blk = ATTN_BLOCK
    tri = (lax.broadcasted_iota(jnp.int32, (blk, blk), 0)
           > lax.broadcasted_iota(jnp.int32, (blk, blk), 1)).astype(BF16)
    q_spec = pl.BlockSpec((None, blk, HEAD_DIM), lambda bi, h, i: (bi, i, h))
    kv_spec = pl.BlockSpec((None, s, HEAD_DIM), lambda bi, h, i: (bi, 0, h))
    return pl.pallas_call(
        functools.partial(_attn_kernel, scale=1.0 / math.sqrt(HEAD_DIM)),
        out_shape=jax.ShapeDtypeStruct(q.shape, BF16),
        grid=(b, N_HEADS, s // blk),
        in_specs=[q_spec, kv_spec, kv_spec, _resident(tri.shape)],
        out_specs=q_spec,
        scratch_shapes=[pltpu.VMEM((blk, HEAD_DIM), F32), pltpu.VMEM((blk, 1), F32)],
        compiler_params=_params(3),
        name="attn",
    )(q, k, v, tri)


def _post_kernel(h1_ref, o_ref, sg_ref, mc_ref, p_ref, wao_ref, wmo_ref, n2_ref, w_in_ref, w_out_ref,
                 pn_ref, wpg_ref, wpp_ref, fn_ref, out_ref, *, final):
    y_attn = _dot(o_ref[...], wao_ref[...])
    merged = mc_ref[...].astype(F32) + sg_ref[...].astype(F32) * y_attn
    h = h1_ref[...] + _dot(merged.astype(BF16), wmo_ref[...])
    h = h + 0.5 * _swiglu(_rms_norm(h, n2_ref[...]).astype(BF16), w_in_ref, w_out_ref)
    ple_gate = jax.nn.sigmoid(_dot(_rms_norm(h, pn_ref[...]).astype(BF16), wpg_ref[...]))
    h = h + ple_gate * _dot(p_ref[...].astype(BF16), wpp_ref[...])
    out_ref[...] = _rms_norm(h, fn_ref[...]) if final else h


def _post(h1, o, sg, mc, p, wao, wmo, n2, w_in, w_out, pn, wpg, wpp, fn, final):
    t, d = h1.shape
    rows = pl.BlockSpec((POST_ROWS, d), lambda i: (i, 0))
    p_rows = pl.BlockSpec((POST_ROWS, p.shape[1]), lambda i: (i, 0))
    weights = (wao, wmo, n2, w_in, w_out, pn, wpg, wpp, fn)
    return pl.pallas_call(
        functools.partial(_post_kernel, final=final),
        out_shape=jax.ShapeDtypeStruct((t, d), F32),
        grid=(t // POST_ROWS,),
        in_specs=[rows, rows, rows, rows, p_rows] + [_resident(w.shape) for w in weights],
        out_specs=rows,
        compiler_params=_params(1),
        name="post",
    )(h1, o, sg, mc, p, *weights)


def kernel(x, p, ffn1_norm, ffn1_w_in, ffn1_w_out, mix_norm, w_mix_in, conv_w, w_conv_out, w_attn_out,
           w_mix_out, ffn2_norm, ffn2_w_in, ffn2_w_out, ple_norm, w_ple_gate, w_ple_proj, final_norm):
    b, s, d = x.shape
    depth = p.shape[0]
    assert d == N_HEADS * HEAD_DIM and conv_w.shape[1] == CONV_K
    assert s % max(FFN_ROWS, MIX_ROWS, POST_ROWS, ATTN_BLOCK) == 0

    def gain(g):
        return g.reshape(1, d).astype(F32)

    h = x.reshape(b * s, d)
    for i in range(depth):
        h1 = _ffn(h, gain(ffn1_norm[i]), ffn1_w_in[i].astype(BF16), ffn1_w_out[i].astype(BF16))
        q, k, v, sg, mc = _mix(h1, gain(mix_norm[i]), w_mix_in[i].astype(BF16), conv_w[i],
                               w_conv_out[i].astype(BF16), s)
        o = _attention(q.reshape(b, s, d), k.reshape(b, s, d), v.reshape(b, s, d))
        h = _post(h1, o.reshape(b * s, d), sg, mc, p[i].reshape(b * s, -1),
                  w_attn_out[i].astype(BF16), w_mix_out[i].astype(BF16), gain(ffn2_norm[i]),
                  ffn2_w_in[i].astype(BF16), ffn2_w_out[i].astype(BF16), gain(ple_norm[i]),
                  w_ple_gate[i].astype(BF16), w_ple_proj[i].astype(BF16), gain(final_norm),
                  final=(i == depth - 1))
    return h.reshape(b, s, d)
```

```python
import functools
import math

import jax
import jax.numpy as jnp
from jax import lax
from jax.experimental import pallas as pl
from jax.experimental.pallas import tpu as pltpu

NORM_EPS = 1e-6
N_HEADS = 8
HEAD_DIM = 128
CONV_K = 3

V7X_VMEM_LIMIT_BYTES = 56 * 1024 * 1024
SUBLANES = 8

FFN_ROWS = 512
MIX_ROWS = 512
POST_ROWS = 256
ATTN_BLOCK = 256

TAIL_EXIT = -104.0
NO_BLOCK_BIAS = -1e30

_dot = functools.partial(jnp.dot, preferred_element_type=jnp.float32)
BF16 = jnp.bfloat16
F32 = jnp.float32


def _rms_norm(x, g):
    ms = jnp.mean(x * x, axis=-1, keepdims=True)
    return x * lax.rsqrt(ms + NORM_EPS) * g


def _swiglu(xn, w_in_ref, w_out_ref):
    d_ff = w_out_ref.shape[0]
    gate = _dot(xn, w_in_ref[:, 0:d_ff])
    up = _dot(xn, w_in_ref[:, d_ff:2 * d_ff])
    act = (gate * jax.nn.sigmoid(gate) * up).astype(BF16)
    return _dot(act, w_out_ref[...])


def _resident(shape):
    return pl.BlockSpec(shape, lambda *_: (0,) * len(shape), pipeline_mode=pl.Buffered(1))


def _params(n_axes):
    return pltpu.CompilerParams(
        dimension_semantics=("arbitrary",) * n_axes,
        vmem_limit_bytes=V7X_VMEM_LIMIT_BYTES,
    )


def _ffn_kernel(x_ref, g_ref, w_in_ref, w_out_ref, o_ref):
    x = x_ref[...]
    xn = _rms_norm(x, g_ref[...]).astype(BF16)
    o_ref[...] = x + 0.5 * _swiglu(xn, w_in_ref, w_out_ref)


def _ffn(x, g, w_in, w_out):
    t, d = x.shape
    rows = pl.BlockSpec((FFN_ROWS, d), lambda i: (i, 0))
    return pl.pallas_call(
        _ffn_kernel,
        out_shape=jax.ShapeDtypeStruct((t, d), F32),
        grid=(t // FFN_ROWS,),
        in_specs=[rows, _resident(g.shape), _resident(w_in.shape), _resident(w_out.shape)],
        out_specs=rows,
        compiler_params=_params(1),
        name="ffn",
    )(x, g, w_in, w_out)


def _mix_kernel(h_ref, g_ref, w_ref, cw_ref, wco_ref,
                q_ref, k_ref, v_ref, sg_ref, mc_ref, cc_buf, *, tiles_per_seq):
    tm, d = h_ref.shape
    u = _rms_norm(h_ref[...], g_ref[...]).astype(BF16)

    def proj(j):
        return _dot(u, w_ref[:, j * d:(j + 1) * d])

    @pl.when(pl.program_id(0) % tiles_per_seq == 0)
    def _():
        cc_buf[0:SUBLANES, :] = jnp.zeros((SUBLANES, d), F32)

    cc = proj(1) * proj(2)
    cc_buf[SUBLANES:SUBLANES + tm, :] = cc
    conv = (cw_ref[0:1, :] * cc_buf[SUBLANES - 2:SUBLANES - 2 + tm, :]
            + cw_ref[1:2, :] * cc_buf[SUBLANES - 1:SUBLANES - 1 + tm, :]
            + cw_ref[2:3, :] * cc)
    cc_buf[0:SUBLANES, :] = cc_buf[tm:tm + SUBLANES, :]

    y_conv = _dot((proj(0) * conv).astype(BF16), wco_ref[...])
    mc_ref[...] = (jax.nn.sigmoid(proj(6)) * y_conv).astype(BF16)
    q_ref[...] = proj(3).astype(BF16)
    k_ref[...] = proj(4).astype(BF16)
    v_ref[...] = proj(5).astype(BF16)
    sg_ref[...] = jax.nn.sigmoid(proj(7)).astype(BF16)


def _mix(h, g, w_mix_in, conv_w, w_conv_out, seq):
    t, d = h.shape
    rows = pl.BlockSpec((MIX_ROWS, d), lambda i: (i, 0))
    out = jax.ShapeDtypeStruct((t, d), BF16)
    return pl.pallas_call(
        functools.partial(_mix_kernel, tiles_per_seq=seq // MIX_ROWS),
        out_shape=(out,) * 5,
        grid=(t // MIX_ROWS,),
        in_specs=[rows, _resident(g.shape), _resident(w_mix_in.shape), _resident(conv_w.shape),
                  _resident(w_conv_out.shape)],
        out_specs=(rows,) * 5,
        scratch_shapes=[pltpu.VMEM((MIX_ROWS + SUBLANES, d), F32)],
        compiler_params=_params(1),
        name="mix",
    )(h, g, w_mix_in, conv_w, w_conv_out)


def _attn_kernel(q_ref, kc_ref, kp_ref, vc_ref, vp_ref, tri2_ref, k_hbm, v_hbm, o_ref,
                 acc_ref, carry_ref, kbuf, vbuf, sems, *, scale):
    blk = q_ref.shape[0]
    b = pl.program_id(0)
    i = pl.program_id(1)
    tri2 = tri2_ref[...]
    causal = (lax.broadcasted_iota(jnp.int32, (blk, blk), 1)
              < lax.broadcasted_iota(jnp.int32, (blk, blk), 0))
    prev_bias = jnp.where(i > 0, 0.0, NO_BLOCK_BIAS)

    def head(ref, h):
        return ref[:, h * HEAD_DIM:(h + 1) * HEAD_DIM]

    def log_sigmoids(q, k_blk):
        z = lax.dot_general(q, k_blk, (((1,), (1,)), ((), ())), preferred_element_type=F32) * scale
        softplus = jnp.maximum(z, 0.0) + jnp.log(1.0 + jnp.exp(-jnp.abs(z)))
        return -softplus, z - softplus

    def tail_sum(l):
        hi = l.astype(BF16)
        lo = (l - hi.astype(F32)).astype(BF16)
        return _dot(jnp.concatenate([hi, lo], axis=1), tri2)

    def row_sum(l):
        return jnp.sum(l, axis=-1, keepdims=True)

    max_carry = None
    for h in range(N_HEADS):
        q = head(q_ref, h)
        log_1m_beta, log_beta = log_sigmoids(q, head(kc_ref, h))
        l_diag = jnp.where(causal, log_1m_beta, 0.0)
        a_diag = jnp.where(causal, jnp.exp(log_beta + tail_sum(l_diag)), 0.0)
        carry = row_sum(l_diag)
        log_1m_beta, log_beta = log_sigmoids(q, head(kp_ref, h))
        a_prev = jnp.exp(log_beta + tail_sum(log_1m_beta) + (carry + prev_bias))
        carry = carry + row_sum(log_1m_beta)
        a = jnp.concatenate([a_diag, a_prev], axis=1).astype(BF16)
        v = jnp.concatenate([head(vc_ref, h), head(vp_ref, h)], axis=0)
        acc_ref[:, h * HEAD_DIM:(h + 1) * HEAD_DIM] = _dot(a, v)
        carry_ref[h] = carry
        head_max = jnp.max(carry)
        max_carry = head_max if max_carry is None else jnp.maximum(max_carry, head_max)

    def more(state):
        j, max_carry = state
        return jnp.logical_and(j >= 0, max_carry > TAIL_EXIT)

    def step(state):
        j, _ = state
        start = pl.multiple_of(j * blk, blk)
        copies = [pltpu.make_async_copy(src.at[b, pl.ds(start, blk), :], dst, sems.at[n])
                  for n, (src, dst) in enumerate(((k_hbm, kbuf), (v_hbm, vbuf)))]
        for c in copies:
            c.start()
        for c in copies:
            c.wait()
        max_carry = None
        for h in range(N_HEADS):
            log_1m_beta, log_beta = log_sigmoids(head(q_ref, h), head(kbuf, h))
            carry = carry_ref[h]
            a = jnp.exp(log_beta + tail_sum(log_1m_beta) + carry)
            acc_ref[:, h * HEAD_DIM:(h + 1) * HEAD_DIM] += _dot(a.astype(BF16), head(vbuf, h))
            carry = carry + row_sum(log_1m_beta)
            carry_ref[h] = carry
            head_max = jnp.max(carry)
            max_carry = head_max if max_carry is None else jnp.maximum(max_carry, head_max)
        return j - 1, max_carry

    lax.while_loop(more, step, (i - 2, max_carry))
    o_ref[...] = acc_ref[...].astype(o_ref.dtype)


def _attention(q, k, v):
    b, s, d = q.shape
    blk = ATTN_BLOCK
    tri = (lax.broadcasted_iota(jnp.int32, (blk, blk), 0)
           > lax.broadcasted_iota(jnp.int32, (blk, blk), 1)).astype(BF16)
    tri2 = jnp.concatenate([tri, tri], axis=0)
    cur = pl.BlockSpec((None, blk, d), lambda bi, i: (bi, i, 0))
    prev = pl.BlockSpec((None, blk, d), lambda bi, i: (bi, jnp.maximum(i - 1, 0), 0))
    hbm = pl.BlockSpec(memory_space=pl.ANY)
    return pl.pallas_call(
        functools.partial(_attn_kernel, scale=1.0 / math.sqrt(HEAD_DIM)),
        out_shape=jax.ShapeDtypeStruct(q.shape, BF16),
        grid=(b, s // blk),
        in_specs=[cur, cur, prev, cur, prev, _resident(tri2.shape), hbm, hbm],
        out_specs=cur,
        scratch_shapes=[pltpu.VMEM((blk, d), F32), pltpu.VMEM((N_HEADS, blk, 1), F32),
                        pltpu.VMEM((blk, d), BF16), pltpu.VMEM((blk, d), BF16),
                        pltpu.SemaphoreType.DMA((2,))],
        compiler_params=_params(2),
        name="attn",
    )(q, k, k, v, v, tri2, k, v)


def _post_kernel(h1_ref, o_ref, sg_ref, mc_ref, p_ref, wao_ref, wmo_ref, n2_ref, w_in_ref, w_out_ref,
                 pn_ref, wpg_ref, wpp_ref, fn_ref, out_ref, *, final):
    y_attn = _dot(o_ref[...], wao_ref[...])
    merged = mc_ref[...].astype(F32) + sg_ref[...].astype(F32) * y_attn
    h = h1_ref[...] + _dot(merged.astype(BF16), wmo_ref[...])
    h = h + 0.5 * _swiglu(_rms_norm(h, n2_ref[...]).astype(BF16), w_in_ref, w_out_ref)
    ple_gate = jax.nn.sigmoid(_dot(_rms_norm(h, pn_ref[...]).astype(BF16), wpg_ref[...]))
    h = h + ple_gate * _dot(p_ref[...].astype(BF16), wpp_ref[...])
    out_ref[...] = _rms_norm(h, fn_ref[...]) if final else h


def _post(h1, o, sg, mc, p, wao, wmo, n2, w_in, w_out, pn, wpg, wpp, fn, final):
    t, d = h1.shape
    rows = pl.BlockSpec((POST_ROWS, d), lambda i: (i, 0))
    p_rows = pl.BlockSpec((POST_ROWS, p.shape[1]), lambda i: (i, 0))
    weights = (wao, wmo, n2, w_in, w_out, pn, wpg, wpp, fn)
    return pl.pallas_call(
        functools.partial(_post_kernel, final=final),
        out_shape=jax.ShapeDtypeStruct((t, d), F32),
        grid=(t // POST_ROWS,),
        in_specs=[rows, rows, rows, rows, p_rows] + [_resident(w.shape) for w in weights],
        out_specs=rows,
        compiler_params=_params(1),
        name="post",
    )(h1, o, sg, mc, p, *weights)


def kernel(x, p, ffn1_norm, ffn1_w_in, ffn1_w_out, mix_norm, w_mix_in, conv_w, w_conv_out, w_attn_out,
           w_mix_out, ffn2_norm, ffn2_w_in, ffn2_w_out, ple_norm, w_ple_gate, w_ple_proj, final_norm):
    b, s, d = x.shape
    depth = p.shape[0]
    assert d == N_HEADS * HEAD_DIM and conv_w.shape[1] == CONV_K
    assert s % max(FFN_ROWS, MIX_ROWS, POST_ROWS, ATTN_BLOCK) == 0

    def gain(g):
        return g.reshape(1, d).astype(F32)

    h = x.reshape(b * s, d)
    for i in range(depth):
        h1 = _ffn(h, gain(ffn1_norm[i]), ffn1_w_in[i].astype(BF16), ffn1_w_out[i].astype(BF16))
        q, k, v, sg, mc = _mix(h1, gain(mix_norm[i]), w_mix_in[i].astype(BF16), conv_w[i],
                               w_conv_out[i].astype(BF16), s)
        o = _attention(q.reshape(b, s, d), k.reshape(b, s, d), v.reshape(b, s, d))
        h = _post(h1, o.reshape(b * s, d), sg, mc, p[i].reshape(b * s, -1),
                  w_attn_out[i].astype(BF16), w_mix_out[i].astype(BF16), gain(ffn2_norm[i]),
                  ffn2_w_in[i].astype(BF16), ffn2_w_out[i].astype(BF16), gain(ple_norm[i]),
                  w_ple_gate[i].astype(BF16), w_ple_proj[i].astype(BF16), gain(final_norm),
                  final=(i == depth - 1))
    return h.reshape(b, s, d)
```

```python
import functools
import math

import jax
import jax.numpy as jnp
from jax import lax
from jax.experimental import pallas as pl
from jax.experimental.pallas import tpu as pltpu

NORM_EPS = 1e-6
N_HEADS = 8
HEAD_DIM = 128
CONV_K = 3

V7X_VMEM_LIMIT_BYTES = 56 * 1024 * 1024
SUBLANES = 8

FFN_ROWS = 512
MIX_ROWS = 512
ATTN_BLOCK = 256

LOG2_E = math.log2(math.e)
TAIL_EXIT_LOG2 = -150.0
NO_BLOCK_BIAS = -1e30

_dot = functools.partial(jnp.dot, preferred_element_type=jnp.float32)
BF16 = jnp.bfloat16
F32 = jnp.float32


def _rms_norm(x, g):
    ms = jnp.mean(x * x, axis=-1, keepdims=True)
    return x * lax.rsqrt(ms + NORM_EPS) * g


def _swiglu(xn, w_in_ref, w_out_ref):
    d_ff = w_out_ref.shape[0]
    gate = _dot(xn, w_in_ref[:, 0:d_ff])
    up = _dot(xn, w_in_ref[:, d_ff:2 * d_ff])
    act = (gate * jax.nn.sigmoid(gate) * up).astype(BF16)
    return _dot(act, w_out_ref[...])


def _resident(shape):
    return pl.BlockSpec(shape, lambda *_: (0,) * len(shape), pipeline_mode=pl.Buffered(1))


def _params(n_axes):
    return pltpu.CompilerParams(
        dimension_semantics=("arbitrary",) * n_axes,
        vmem_limit_bytes=V7X_VMEM_LIMIT_BYTES,
    )


def _ffn_kernel(x_ref, g_ref, w_in_ref, w_out_ref, o_ref):
    x = x_ref[...]
    xn = _rms_norm(x, g_ref[...]).astype(BF16)
    o_ref[...] = x + 0.5 * _swiglu(xn, w_in_ref, w_out_ref)


def _ffn(x, g, w_in, w_out):
    t, d = x.shape
    rows = pl.BlockSpec((FFN_ROWS, d), lambda i: (i, 0))
    return pl.pallas_call(
        _ffn_kernel,
        out_shape=jax.ShapeDtypeStruct((t, d), F32),
        grid=(t // FFN_ROWS,),
        in_specs=[rows, _resident(g.shape), _resident(w_in.shape), _resident(w_out.shape)],
        out_specs=rows,
        compiler_params=_params(1),
        name="ffn",
    )(x, g, w_in, w_out)


def _mix_kernel(h_ref, g_ref, w_ref, cw_ref, wco_ref,
                q_ref, k_ref, v_ref, sg_ref, mc_ref, cc_buf, *, tiles_per_seq):
    tm, d = h_ref.shape
    u = _rms_norm(h_ref[...], g_ref[...]).astype(BF16)

    def proj(j):
        return _dot(u, w_ref[:, j * d:(j + 1) * d])

    @pl.when(pl.program_id(0) % tiles_per_seq == 0)
    def _():
        cc_buf[0:SUBLANES, :] = jnp.zeros((SUBLANES, d), F32)

    cc = proj(1) * proj(2)
    cc_buf[SUBLANES:SUBLANES + tm, :] = cc
    conv = (cw_ref[0:1, :] * cc_buf[SUBLANES - 2:SUBLANES - 2 + tm, :]
            + cw_ref[1:2, :] * cc_buf[SUBLANES - 1:SUBLANES - 1 + tm, :]
            + cw_ref[2:3, :] * cc)
    cc_buf[0:SUBLANES, :] = cc_buf[tm:tm + SUBLANES, :]

    y_conv = _dot((proj(0) * conv).astype(BF16), wco_ref[...])
    mc_ref[...] = (jax.nn.sigmoid(proj(6)) * y_conv).astype(BF16)
    sg_ref[...] = jax.nn.sigmoid(proj(7)).astype(BF16)
    q_ref[...] = proj(3).astype(BF16)
    k_ref[...] = proj(4).astype(BF16)
    v_ref[...] = proj(5).astype(BF16)


def _mix(h, g, w_mix_in, conv_w, w_conv_out, seq):
    t, d = h.shape
    rows = pl.BlockSpec((MIX_ROWS, d), lambda i: (i, 0))
    out = jax.ShapeDtypeStruct((t, d), BF16)
    return pl.pallas_call(
        functools.partial(_mix_kernel, tiles_per_seq=seq // MIX_ROWS),
        out_shape=(out,) * 5,
        grid=(t // MIX_ROWS,),
        in_specs=[rows, _resident(g.shape), _resident(w_mix_in.shape), _resident(conv_w.shape),
                  _resident(w_conv_out.shape)],
        out_specs=(rows,) * 5,
        scratch_shapes=[pltpu.VMEM((MIX_ROWS + SUBLANES, d), F32)],
        compiler_params=_params(1),
        name="mix",
    )(h, g, w_mix_in, conv_w, w_conv_out)


FFN2_CHUNKS = ((0, 768), (768, 1536), (1536, 2304), (2304, 2816))


def _dense_tail_stages(o, h1_ref, sg_ref, mc_ref, p_ref, wao_ref, wmo_ref, n2_ref, w_in_ref, w_out_ref,
                       pn_ref, wpg_ref, wpp_ref, fn_ref, out_ref, final):
    d_ff = w_out_ref.shape[0]
    assert FFN2_CHUNKS[0][0] == 0 and FFN2_CHUNKS[-1][1] == d_ff
    y_attn = _dot(o, wao_ref[...])
    merged = (mc_ref[...].astype(F32) + sg_ref[...].astype(F32) * y_attn).astype(BF16)
    yield
    h = h1_ref[...] + _dot(merged, wmo_ref[...])
    xn = _rms_norm(h, n2_ref[...]).astype(BF16)
    yield
    y = None
    for lo, hi in FFN2_CHUNKS:
        gate = _dot(xn, w_in_ref[:, lo:hi])
        up = _dot(xn, w_in_ref[:, d_ff + lo:d_ff + hi])
        act = (gate * jax.nn.sigmoid(gate) * up).astype(BF16)
        yield
        part = _dot(act, w_out_ref[lo:hi, :])
        y = part if y is None else y + part
        yield
    h = h + 0.5 * y
    ple_gate = jax.nn.sigmoid(_dot(_rms_norm(h, pn_ref[...]).astype(BF16), wpg_ref[...]))
    h = h + ple_gate * _dot(p_ref[...].astype(BF16), wpp_ref[...])
    out_ref[...] = _rms_norm(h, fn_ref[...]) if final else h


def _attn_post_kernel(q_ref, kc_ref, kp_ref, vc_ref, vp_ref, tri2_ref, k_hbm, v_hbm,
                      h1_ref, sg_ref, mc_ref, p_ref, wao_ref, wmo_ref, n2_ref, w_in_ref, w_out_ref,
                      pn_ref, wpg_ref, wpp_ref, fn_ref, out_ref,
                      o_scr, acc_ref, carry_ref, kbuf, vbuf, sems, *, blocks_per_seq, n_tiles, final):
    blk = q_ref.shape[0]
    s = pl.program_id(0)

    @pl.when(s == 0)
    def _():
        o_scr[...] = jnp.zeros_like(o_scr)

    dense = _dense_tail_stages(o_scr[...], h1_ref, sg_ref, mc_ref, p_ref, wao_ref, wmo_ref, n2_ref,
                               w_in_ref, w_out_ref, pn_ref, wpg_ref, wpp_ref, fn_ref, out_ref, final)

    tile = jnp.minimum(s, n_tiles - 1)
    i = lax.rem(tile, blocks_per_seq)
    seq_start = (tile - i) * blk
    tri2 = tri2_ref[...]
    causal = (lax.broadcasted_iota(jnp.int32, (blk, blk), 1)
              < lax.broadcasted_iota(jnp.int32, (blk, blk), 0))
    prev_bias = jnp.where(i > 0, 0.0, NO_BLOCK_BIAS)
    scale_log2 = LOG2_E / math.sqrt(HEAD_DIM)

    def head(ref, h):
        return ref[:, h * HEAD_DIM:(h + 1) * HEAD_DIM]

    def log2_scores(q, k_blk):
        return lax.dot_general(q, k_blk, (((1,), (1,)), ((), ())), preferred_element_type=F32) * scale_log2

    def softplus2(z2):
        return jnp.maximum(z2, 0.0) + jnp.log(1.0 + jnp.exp2(-jnp.abs(z2))) * LOG2_E

    def tail_sum(x):
        hi = x.astype(BF16)
        lo = (x - hi.astype(F32)).astype(BF16)
        return _dot(jnp.concatenate([hi, lo], axis=1), tri2)

    def row_sum(x):
        return jnp.sum(x, axis=-1, keepdims=True)

    carries = []

    def attend(h):
        q = head(q_ref, h)
        z2_diag = log2_scores(q, head(kc_ref, h))
        z2_prev = log2_scores(q, head(kp_ref, h))
        yield
        sp2_diag = jnp.where(causal, softplus2(z2_diag), 0.0)
        sp2_prev = softplus2(z2_prev)
        tail_diag = tail_sum(sp2_diag)
        tail_prev = tail_sum(sp2_prev)
        yield
        carry = -row_sum(sp2_diag)
        a_diag = jnp.where(causal, jnp.exp2(z2_diag - sp2_diag - tail_diag), 0.0)
        a_prev = jnp.exp2(z2_prev - sp2_prev - tail_prev + (carry + prev_bias))
        carry = carry - row_sum(sp2_prev)
        a = jnp.concatenate([a_diag, a_prev], axis=1).astype(BF16)
        v = jnp.concatenate([head(vc_ref, h), head(vp_ref, h)], axis=0)
        acc_ref[:, h * HEAD_DIM:(h + 1) * HEAD_DIM] = _dot(a, v)
        carry_ref[h] = carry
        carries.append(carry)

    heads = [attend(h) for h in range(N_HEADS)]
    live = [dense] + heads
    slot = 0
    while live:
        for gen in [dense] + heads[max(slot - 2, 0):slot + 1]:
            if gen in live and next(gen, "done") == "done":
                live.remove(gen)
        slot += 1
    max_carry = functools.reduce(jnp.maximum, carries)

    def more(state):
        j, max_carry = state
        return jnp.logical_and(j >= 0, max_carry > TAIL_EXIT_LOG2)

    def step(state):
        j, _ = state
        start = pl.multiple_of(seq_start + j * blk, blk)
        copies = [pltpu.make_async_copy(src.at[pl.ds(start, blk), :], dst, sems.at[n])
                  for n, (src, dst) in enumerate(((k_hbm, kbuf), (v_hbm, vbuf)))]
        for c in copies:
            c.start()
        for c in copies:
            c.wait()
        max_carry = None
        for h in range(N_HEADS):
            z2 = log2_scores(head(q_ref, h), head(kbuf, h))
            sp2 = softplus2(z2)
            carry = carry_ref[h]
            a = jnp.exp2(z2 - sp2 - tail_sum(sp2) + carry)
            acc_ref[:, h * HEAD_DIM:(h + 1) * HEAD_DIM] += _dot(a.astype(BF16), head(vbuf, h))
            carry = carry - row_sum(sp2)
            carry_ref[h] = carry
            max_carry = carry if max_carry is None else jnp.maximum(max_carry, carry)
        return j - 1, jnp.max(max_carry)

    lax.while_loop(more, step, (i - 2, jnp.max(max_carry)))
    o_scr[...] = acc_ref[...].astype(o_scr.dtype)


def _attn_post(q, k, v, h1, sg, mc, p, wao, wmo, n2, w_in, w_out, pn, wpg, wpp, fn, seq, final):
    t, d = h1.shape
    blk = ATTN_BLOCK
    n_tiles = t // blk
    tri = (lax.broadcasted_iota(jnp.int32, (blk, blk), 0)
           > lax.broadcasted_iota(jnp.int32, (blk, blk), 1)).astype(BF16)
    tri2 = jnp.concatenate([tri, tri], axis=0)

    def attn_tile(s):
        return jnp.minimum(s, n_tiles - 1)

    cur = pl.BlockSpec((blk, d), lambda s: (attn_tile(s), 0))
    prev = pl.BlockSpec((blk, d), lambda s: (jnp.maximum(attn_tile(s) - 1, 0), 0))
    hbm = pl.BlockSpec(memory_space=pl.ANY)
    rows = pl.BlockSpec((blk, d), lambda s: (jnp.maximum(s - 1, 0), 0))
    p_rows = pl.BlockSpec((blk, p.shape[1]), lambda s: (jnp.maximum(s - 1, 0), 0))
    weights = (wao, wmo, n2, w_in, w_out, pn, wpg, wpp, fn)
    return pl.pallas_call(
        functools.partial(_attn_post_kernel, blocks_per_seq=seq // blk, n_tiles=n_tiles, final=final),
        out_shape=jax.ShapeDtypeStruct((t, d), F32),
        grid=(n_tiles + 1,),
        in_specs=[cur, cur, prev, cur, prev, _resident(tri2.shape), hbm, hbm, rows, rows, rows, p_rows]
                 + [_resident(w.shape) for w in weights],
        out_specs=rows,
        scratch_shapes=[pltpu.VMEM((blk, d), BF16), pltpu.VMEM((blk, d), F32),
                        pltpu.VMEM((N_HEADS, blk, 1), F32),
                        pltpu.VMEM((blk, d), BF16), pltpu.VMEM((blk, d), BF16),
                        pltpu.SemaphoreType.DMA((2,))],
        compiler_params=_params(1),
        name="attn_post",
    )(q, k, k, v, v, tri2, k, v, h1, sg, mc, p, *weights)


def kernel(x, p, ffn1_norm, ffn1_w_in, ffn1_w_out, mix_norm, w_mix_in, conv_w, w_conv_out, w_attn_out,
           w_mix_out, ffn2_norm, ffn2_w_in, ffn2_w_out, ple_norm, w_ple_gate, w_ple_proj, final_norm):
    b, s, d = x.shape
    depth = p.shape[0]
    assert d == N_HEADS * HEAD_DIM and conv_w.shape[1] == CONV_K
    assert s % max(FFN_ROWS, MIX_ROWS, ATTN_BLOCK) == 0

    def gain(g):
        return g.reshape(1, d).astype(F32)

    h = x.reshape(b * s, d)
    for i in range(depth):
        h1 = _ffn(h, gain(ffn1_norm[i]), ffn1_w_in[i].astype(BF16), ffn1_w_out[i].astype(BF16))
        q, k, v, sg, mc = _mix(h1, gain(mix_norm[i]), w_mix_in[i].astype(BF16), conv_w[i],
                               w_conv_out[i].astype(BF16), s)
        h = _attn_post(q, k, v, h1, sg, mc, p[i].reshape(b * s, -1),
                       w_attn_out[i].astype(BF16), w_mix_out[i].astype(BF16), gain(ffn2_norm[i]),
                       ffn2_w_in[i].astype(BF16), ffn2_w_out[i].astype(BF16), gain(ple_norm[i]),
                       w_ple_gate[i].astype(BF16), w_ple_proj[i].astype(BF16), gain(final_norm),
                       seq=s, final=(i == depth - 1))
    return h.reshape(b, s, d)
```

```python
import functools
import math

import jax
import jax.numpy as jnp
from jax import lax
from jax.experimental import pallas as pl
from jax.experimental.pallas import tpu as pltpu

NORM_EPS = 1e-6
N_HEADS = 8
HEAD_DIM = 128
CONV_K = 3

V7X_VMEM_LIMIT_BYTES = 56 * 1024 * 1024
SUBLANES = 8

FFN_ROWS = 512
MIX_ROWS = 512
ATTN_BLOCK = 256

LOG2_E = math.log2(math.e)
TAIL_EXIT_LOG2 = -150.0
NO_BLOCK_BIAS = -1e30

_dot = functools.partial(jnp.dot, preferred_element_type=jnp.float32)
BF16 = jnp.bfloat16
F32 = jnp.float32


def _rms_norm(x, g):
    ms = jnp.mean(x * x, axis=-1, keepdims=True)
    return x * lax.rsqrt(ms + NORM_EPS) * g


def _swiglu(xn, w_in_ref, w_out_ref):
    d_ff = w_out_ref.shape[0]
    gate = _dot(xn, w_in_ref[:, 0:d_ff])
    up = _dot(xn, w_in_ref[:, d_ff:2 * d_ff])
    act = (gate * jax.nn.sigmoid(gate) * up).astype(BF16)
    return _dot(act, w_out_ref[...])


def _resident(shape):
    return pl.BlockSpec(shape, lambda *_: (0,) * len(shape), pipeline_mode=pl.Buffered(1))


def _params(n_axes):
    return pltpu.CompilerParams(
        dimension_semantics=("arbitrary",) * n_axes,
        vmem_limit_bytes=V7X_VMEM_LIMIT_BYTES,
    )


def _ffn_kernel(x_ref, g_ref, w_in_ref, w_out_ref, o_ref):
    x = x_ref[...]
    xn = _rms_norm(x, g_ref[...]).astype(BF16)
    o_ref[...] = x + 0.5 * _swiglu(xn, w_in_ref, w_out_ref)


def _ffn(x, g, w_in, w_out):
    t, d = x.shape
    rows = pl.BlockSpec((FFN_ROWS, d), lambda i: (i, 0))
    return pl.pallas_call(
        _ffn_kernel,
        out_shape=jax.ShapeDtypeStruct((t, d), F32),
        grid=(t // FFN_ROWS,),
        in_specs=[rows, _resident(g.shape), _resident(w_in.shape), _resident(w_out.shape)],
        out_specs=rows,
        compiler_params=_params(1),
        name="ffn",
    )(x, g, w_in, w_out)


def _mix_kernel(h_ref, g_ref, w_ref, cw_ref, wco_ref, qkv_ref, gates_ref, cc_buf, *, tiles_per_seq):
    tm, d = h_ref.shape
    u = _rms_norm(h_ref[...], g_ref[...]).astype(BF16)

    def proj(j):
        return _dot(u, w_ref[:, j * d:(j + 1) * d])

    @pl.when(pl.program_id(0) % tiles_per_seq == 0)
    def _():
        cc_buf[0:SUBLANES, :] = jnp.zeros((SUBLANES, d), F32)

    cc = proj(1) * proj(2)
    cc_buf[SUBLANES:SUBLANES + tm, :] = cc
    conv = (cw_ref[0:1, :] * cc_buf[SUBLANES - 2:SUBLANES - 2 + tm, :]
            + cw_ref[1:2, :] * cc_buf[SUBLANES - 1:SUBLANES - 1 + tm, :]
            + cw_ref[2:3, :] * cc)
    cc_buf[0:SUBLANES, :] = cc_buf[tm:tm + SUBLANES, :]

    y_conv = _dot((proj(0) * conv).astype(BF16), wco_ref[...])
    gates_ref[:, d:2 * d] = (jax.nn.sigmoid(proj(6)) * y_conv).astype(BF16)
    gates_ref[:, 0:d] = jax.nn.sigmoid(proj(7)).astype(BF16)
    for j in range(3):
        qkv_ref[:, j * d:(j + 1) * d] = proj(3 + j).astype(BF16)


def _mix(h, g, w_mix_in, conv_w, w_conv_out, seq):
    t, d = h.shape
    rows = pl.BlockSpec((MIX_ROWS, d), lambda i: (i, 0))
    return pl.pallas_call(
        functools.partial(_mix_kernel, tiles_per_seq=seq // MIX_ROWS),
        out_shape=(jax.ShapeDtypeStruct((t, 3 * d), BF16), jax.ShapeDtypeStruct((t, 2 * d), BF16)),
        grid=(t // MIX_ROWS,),
        in_specs=[rows, _resident(g.shape), _resident(w_mix_in.shape), _resident(conv_w.shape),
                  _resident(w_conv_out.shape)],
        out_specs=(pl.BlockSpec((MIX_ROWS, 3 * d), lambda i: (i, 0)),
                   pl.BlockSpec((MIX_ROWS, 2 * d), lambda i: (i, 0))),
        scratch_shapes=[pltpu.VMEM((MIX_ROWS + SUBLANES, d), F32)],
        compiler_params=_params(1),
        name="mix",
    )(h, g, w_mix_in, conv_w, w_conv_out)


FFN2_CHUNKS = ((0, 768), (768, 1536), (1536, 2304), (2304, 2816))


def _dense_tail_stages(o, h1_ref, gates_ref, p_ref, wao_ref, wmo_ref, n2_ref, w_in_ref, w_out_ref,
                       pn_ref, wpg_ref, wpp_ref, fn_ref, out_ref, final):
    d_ff = w_out_ref.shape[0]
    assert FFN2_CHUNKS[0][0] == 0 and FFN2_CHUNKS[-1][1] == d_ff
    d = o.shape[1]
    y_attn = _dot(o, wao_ref[...])
    merged = (gates_ref[:, d:2 * d].astype(F32) + gates_ref[:, 0:d].astype(F32) * y_attn).astype(BF16)
    yield
    h = h1_ref[...] + _dot(merged, wmo_ref[...])
    xn = _rms_norm(h, n2_ref[...]).astype(BF16)
    yield
    y = None
    for lo, hi in FFN2_CHUNKS:
        gate = _dot(xn, w_in_ref[:, lo:hi])
        up = _dot(xn, w_in_ref[:, d_ff + lo:d_ff + hi])
        act = (gate * jax.nn.sigmoid(gate) * up).astype(BF16)
        yield
        part = _dot(act, w_out_ref[lo:hi, :])
        y = part if y is None else y + part
        yield
    h = h + 0.5 * y
    ple_gate = jax.nn.sigmoid(_dot(_rms_norm(h, pn_ref[...]).astype(BF16), wpg_ref[...]))
    h = h + ple_gate * _dot(p_ref[...].astype(BF16), wpp_ref[...])
    out_ref[...] = _rms_norm(h, fn_ref[...]) if final else h


def _attn_post_kernel(qkv_ref, tri2_ref, qkv_hbm, h1_ref, gates_ref, p_ref, wao_ref, wmo_ref, n2_ref,
                      w_in_ref, w_out_ref, pn_ref, wpg_ref, wpp_ref, fn_ref, out_ref,
                      o_scr, kv_prev, acc_ref, carry_ref, kv_buf, sem, *, blocks_per_seq, n_tiles, final):
    blk, d = o_scr.shape
    s = pl.program_id(0)

    @pl.when(s == 0)
    def _():
        o_scr[...] = jnp.zeros_like(o_scr)
        kv_prev[...] = jnp.zeros_like(kv_prev)

    dense = _dense_tail_stages(o_scr[...], h1_ref, gates_ref, p_ref, wao_ref, wmo_ref, n2_ref,
                               w_in_ref, w_out_ref, pn_ref, wpg_ref, wpp_ref, fn_ref, out_ref, final)

    tile = jnp.minimum(s, n_tiles - 1)
    i = lax.rem(tile, blocks_per_seq)
    seq_start = (tile - i) * blk
    tri2 = tri2_ref[...]
    causal = (lax.broadcasted_iota(jnp.int32, (blk, blk), 1)
              < lax.broadcasted_iota(jnp.int32, (blk, blk), 0))
    prev_bias = jnp.where(i > 0, 0.0, NO_BLOCK_BIAS)
    scale_log2 = LOG2_E / math.sqrt(HEAD_DIM)

    def head(ref, part, h):
        return ref[:, part * d + h * HEAD_DIM:part * d + (h + 1) * HEAD_DIM]

    def log2_scores(q, k_blk):
        return lax.dot_general(q, k_blk, (((1,), (1,)), ((), ())), preferred_element_type=F32) * scale_log2

    def softplus2(z2):
        return jnp.maximum(z2, 0.0) + jnp.log(1.0 + jnp.exp2(-jnp.abs(z2))) * LOG2_E

    def tail_sum(x):
        hi = x.astype(BF16)
        lo = (x - hi.astype(F32)).astype(BF16)
        return _dot(jnp.concatenate([hi, lo], axis=1), tri2)

    def row_sum(x):
        return jnp.sum(x, axis=-1, keepdims=True)

    carries = []

    def attend(h):
        q = head(qkv_ref, 0, h)
        z2_diag = log2_scores(q, head(qkv_ref, 1, h))
        z2_prev = log2_scores(q, head(kv_prev, 0, h))
        yield
        sp2_diag = jnp.where(causal, softplus2(z2_diag), 0.0)
        sp2_prev = softplus2(z2_prev)
        tail_diag = tail_sum(sp2_diag)
        tail_prev = tail_sum(sp2_prev)
        yield
        carry = -row_sum(sp2_diag)
        a_diag = jnp.where(causal, jnp.exp2(z2_diag - sp2_diag - tail_diag), 0.0)
        a_prev = jnp.exp2(z2_prev - sp2_prev - tail_prev + (carry + prev_bias))
        carry = carry - row_sum(sp2_prev)
        a = jnp.concatenate([a_diag, a_prev], axis=1).astype(BF16)
        v = jnp.concatenate([head(qkv_ref, 2, h), head(kv_prev, 1, h)], axis=0)
        acc_ref[:, h * HEAD_DIM:(h + 1) * HEAD_DIM] = _dot(a, v)
        carry_ref[h] = carry
        carries.append(carry)

    heads = [attend(h) for h in range(N_HEADS)]
    live = [dense] + heads
    slot = 0
    while live:
        for gen in [dense] + heads[max(slot - 2, 0):slot + 1]:
            if gen in live and next(gen, "done") == "done":
                live.remove(gen)
        slot += 1
    max_carry = functools.reduce(jnp.maximum, carries)

    def more(state):
        j, max_carry = state
        return jnp.logical_and(j >= 0, max_carry > TAIL_EXIT_LOG2)

    def step(state):
        j, _ = state
        start = pl.multiple_of(seq_start + j * blk, blk)
        copy = pltpu.make_async_copy(qkv_hbm.at[pl.ds(start, blk), pl.ds(d, 2 * d)], kv_buf, sem)
        copy.start()
        copy.wait()
        max_carry = None
        for h in range(N_HEADS):
            z2 = log2_scores(head(qkv_ref, 0, h), head(kv_buf, 0, h))
            sp2 = softplus2(z2)
            carry = carry_ref[h]
            a = jnp.exp2(z2 - sp2 - tail_sum(sp2) + carry)
            acc_ref[:, h * HEAD_DIM:(h + 1) * HEAD_DIM] += _dot(a.astype(BF16), head(kv_buf, 1, h))
            carry = carry - row_sum(sp2)
            carry_ref[h] = carry
            max_carry = carry if max_carry is None else jnp.maximum(max_carry, carry)
        return j - 1, jnp.max(max_carry)

    lax.while_loop(more, step, (i - 2, jnp.max(max_carry)))
    o_scr[...] = acc_ref[...].astype(o_scr.dtype)
    kv_prev[...] = qkv_ref[:, d:3 * d]


def _attn_post(qkv, gates, h1, p, wao, wmo, n2, w_in, w_out, pn, wpg, wpp, fn, seq, final):
    t, d = h1.shape
    blk = ATTN_BLOCK
    n_tiles = t // blk
    tri = (lax.broadcasted_iota(jnp.int32, (blk, blk), 0)
           > lax.broadcasted_iota(jnp.int32, (blk, blk), 1)).astype(BF16)
    tri2 = jnp.concatenate([tri, tri], axis=0)

    def attn_rows(width):
        return pl.BlockSpec((blk, width), lambda s: (jnp.minimum(s, n_tiles - 1), 0))

    def tail_rows(width):
        return pl.BlockSpec((blk, width), lambda s: (jnp.maximum(s - 1, 0), 0))

    hbm = pl.BlockSpec(memory_space=pl.ANY)
    weights = (wao, wmo, n2, w_in, w_out, pn, wpg, wpp, fn)
    return pl.pallas_call(
        functools.partial(_attn_post_kernel, blocks_per_seq=seq // blk, n_tiles=n_tiles, final=final),
        out_shape=jax.ShapeDtypeStruct((t, d), F32),
        grid=(n_tiles + 1,),
        in_specs=[attn_rows(3 * d), _resident(tri2.shape), hbm, tail_rows(d), tail_rows(2 * d),
                  tail_rows(p.shape[1])] + [_resident(w.shape) for w in weights],
        out_specs=tail_rows(d),
        scratch_shapes=[pltpu.VMEM((blk, d), BF16), pltpu.VMEM((blk, 2 * d), BF16),
                        pltpu.VMEM((blk, d), F32), pltpu.VMEM((N_HEADS, blk, 1), F32),
                        pltpu.VMEM((blk, 2 * d), BF16), pltpu.SemaphoreType.DMA(())],
        compiler_params=_params(1),
        name="attn_post",
    )(qkv, tri2, qkv, h1, gates, p, *weights)


def kernel(x, p, ffn1_norm, ffn1_w_in, ffn1_w_out, mix_norm, w_mix_in, conv_w, w_conv_out, w_attn_out,
           w_mix_out, ffn2_norm, ffn2_w_in, ffn2_w_out, ple_norm, w_ple_gate, w_ple_proj, final_norm):
    b, s, d = x.shape
    depth = p.shape[0]
    assert d == N_HEADS * HEAD_DIM and conv_w.shape[1] == CONV_K
    assert s % max(FFN_ROWS, MIX_ROWS, ATTN_BLOCK) == 0

    def gain(g):
        return g.reshape(1, d).astype(F32)

    h = x.reshape(b * s, d)
    for i in range(depth):
        h1 = _ffn(h, gain(ffn1_norm[i]), ffn1_w_in[i].astype(BF16), ffn1_w_out[i].astype(BF16))
        qkv, gates = _mix(h1, gain(mix_norm[i]), w_mix_in[i].astype(BF16), conv_w[i],
                          w_conv_out[i].astype(BF16), s)
        h = _attn_post(qkv, gates, h1, p[i].reshape(b * s, -1),
                       w_attn_out[i].astype(BF16), w_mix_out[i].astype(BF16), gain(ffn2_norm[i]),
                       ffn2_w_in[i].astype(BF16), ffn2_w_out[i].astype(BF16), gain(ple_norm[i]),
                       w_ple_gate[i].astype(BF16), w_ple_proj[i].astype(BF16), gain(final_norm),
                       seq=s, final=(i == depth - 1))
    return h.reshape(b, s, d)
```

```python
import functools
import math

import jax
import jax.numpy as jnp
from jax import lax
from jax.experimental import pallas as pl
from jax.experimental.pallas import tpu as pltpu

NORM_EPS = 1e-6
N_HEADS = 8
HEAD_DIM = 128
CONV_K = 3

V7X_VMEM_LIMIT_BYTES = 56 * 1024 * 1024
SUBLANES = 8

FFN_ROWS = 512
MIX_ROWS = 512
ATTN_BLOCK = 256

LOG2_E = math.log2(math.e)
TAIL_EXIT_LOG2 = -150.0
NO_BLOCK_BIAS = -1e30

_dot = functools.partial(jnp.dot, preferred_element_type=jnp.float32)
BF16 = jnp.bfloat16
F32 = jnp.float32


def _rms_norm(x, g):
    ms = jnp.mean(x * x, axis=-1, keepdims=True)
    return x * lax.rsqrt(ms + NORM_EPS) * g


def _swiglu(xn, w_in_ref, w_out_ref):
    d_ff = w_out_ref.shape[0]
    gate = _dot(xn, w_in_ref[:, 0:d_ff])
    up = _dot(xn, w_in_ref[:, d_ff:2 * d_ff])
    act = (gate * jax.nn.sigmoid(gate) * up).astype(BF16)
    return _dot(act, w_out_ref[...])


def _resident(shape):
    return pl.BlockSpec(shape, lambda *_: (0,) * len(shape), pipeline_mode=pl.Buffered(1))


def _params(n_axes):
    return pltpu.CompilerParams(
        dimension_semantics=("arbitrary",) * n_axes,
        vmem_limit_bytes=V7X_VMEM_LIMIT_BYTES,
    )


def _ffn_kernel(x_ref, g_ref, w_in_ref, w_out_ref, o_ref):
    x = x_ref[...]
    xn = _rms_norm(x, g_ref[...]).astype(BF16)
    o_ref[...] = x + 0.5 * _swiglu(xn, w_in_ref, w_out_ref)


def _ffn(x, g, w_in, w_out):
    t, d = x.shape
    rows = pl.BlockSpec((FFN_ROWS, d), lambda i: (i, 0))
    return pl.pallas_call(
        _ffn_kernel,
        out_shape=jax.ShapeDtypeStruct((t, d), F32),
        grid=(t // FFN_ROWS,),
        in_specs=[rows, _resident(g.shape), _resident(w_in.shape), _resident(w_out.shape)],
        out_specs=rows,
        compiler_params=_params(1),
        name="ffn",
    )(x, g, w_in, w_out)


def _mix_kernel(h_ref, g_ref, w_ref, cw_ref, wco_ref, qkv_ref, gates_ref, cc_buf, *, tiles_per_seq):
    tm, d = h_ref.shape
    u = _rms_norm(h_ref[...], g_ref[...]).astype(BF16)

    def proj(j):
        return _dot(u, w_ref[:, j * d:(j + 1) * d])

    @pl.when(pl.program_id(0) % tiles_per_seq == 0)
    def _():
        cc_buf[0:SUBLANES, :] = jnp.zeros((SUBLANES, d), F32)

    cc = proj(1) * proj(2)
    cc_buf[SUBLANES:SUBLANES + tm, :] = cc
    c_b = proj(0)
    qkv_ref[:, 0:d] = proj(3).astype(BF16)
    conv = (cw_ref[0:1, :] * cc_buf[SUBLANES - 2:SUBLANES - 2 + tm, :]
            + cw_ref[1:2, :] * cc_buf[SUBLANES - 1:SUBLANES - 1 + tm, :]
            + cw_ref[2:3, :] * cc)
    cc_buf[0:SUBLANES, :] = cc_buf[tm:tm + SUBLANES, :]
    conv_gate = jax.nn.sigmoid(proj(6))
    gates_ref[:, 0:d] = jax.nn.sigmoid(proj(7)).astype(BF16)
    y_conv = _dot((c_b * conv).astype(BF16), wco_ref[...])
    qkv_ref[:, d:2 * d] = proj(4).astype(BF16)
    gates_ref[:, d:2 * d] = (conv_gate * y_conv).astype(BF16)
    qkv_ref[:, 2 * d:3 * d] = proj(5).astype(BF16)


def _mix(h, g, w_mix_in, conv_w, w_conv_out, seq):
    t, d = h.shape
    rows = pl.BlockSpec((MIX_ROWS, d), lambda i: (i, 0))
    return pl.pallas_call(
        functools.partial(_mix_kernel, tiles_per_seq=seq // MIX_ROWS),
        out_shape=(jax.ShapeDtypeStruct((t, 3 * d), BF16), jax.ShapeDtypeStruct((t, 2 * d), BF16)),
        grid=(t // MIX_ROWS,),
        in_specs=[rows, _resident(g.shape), _resident(w_mix_in.shape), _resident(conv_w.shape),
                  _resident(w_conv_out.shape)],
        out_specs=(pl.BlockSpec((MIX_ROWS, 3 * d), lambda i: (i, 0)),
                   pl.BlockSpec((MIX_ROWS, 2 * d), lambda i: (i, 0))),
        scratch_shapes=[pltpu.VMEM((MIX_ROWS + SUBLANES, d), F32)],
        compiler_params=_params(1),
        name="mix",
    )(h, g, w_mix_in, conv_w, w_conv_out)


FFN2_CHUNKS = ((0, 768), (768, 1536), (1536, 2304), (2304, 2816))


def _dense_tail_stages(o, h1_ref, gates_ref, p_ref, wao_ref, wmo_ref, n2_ref, w_in_ref, w_out_ref,
                       pn_ref, wpg_ref, wpp_ref, fn_ref, out_ref, final):
    d_ff = w_out_ref.shape[0]
    assert FFN2_CHUNKS[0][0] == 0 and FFN2_CHUNKS[-1][1] == d_ff
    d = o.shape[1]
    y_attn = _dot(o, wao_ref[...])
    merged = (gates_ref[:, d:2 * d].astype(F32) + gates_ref[:, 0:d].astype(F32) * y_attn).astype(BF16)
    yield
    h = h1_ref[...] + _dot(merged, wmo_ref[...])
    xn = _rms_norm(h, n2_ref[...]).astype(BF16)
    yield
    y = None
    for lo, hi in FFN2_CHUNKS:
        gate = _dot(xn, w_in_ref[:, lo:hi])
        up = _dot(xn, w_in_ref[:, d_ff + lo:d_ff + hi])
        act = (gate * jax.nn.sigmoid(gate) * up).astype(BF16)
        yield
        part = _dot(act, w_out_ref[lo:hi, :])
        y = part if y is None else y + part
        yield
    h = h + 0.5 * y
    ple_gate = jax.nn.sigmoid(_dot(_rms_norm(h, pn_ref[...]).astype(BF16), wpg_ref[...]))
    h = h + ple_gate * _dot(p_ref[...].astype(BF16), wpp_ref[...])
    out_ref[...] = _rms_norm(h, fn_ref[...]) if final else h


def _attn_post_kernel(qkv_ref, tri2_ref, qkv_hbm, h1_ref, gates_ref, p_ref, wao_ref, wmo_ref, n2_ref,
                      w_in_ref, w_out_ref, pn_ref, wpg_ref, wpp_ref, fn_ref, out_ref,
                      o_scr, kv_prev, acc_ref, carry_ref, kv_buf, sem, *, blocks_per_seq, n_tiles, final):
    blk, d = o_scr.shape
    s = pl.program_id(0)

    @pl.when(s == 0)
    def _():
        o_scr[...] = jnp.zeros_like(o_scr)
        kv_prev[...] = jnp.zeros_like(kv_prev)

    dense = _dense_tail_stages(o_scr[...], h1_ref, gates_ref, p_ref, wao_ref, wmo_ref, n2_ref,
                               w_in_ref, w_out_ref, pn_ref, wpg_ref, wpp_ref, fn_ref, out_ref, final)

    tile = jnp.minimum(s, n_tiles - 1)
    i = lax.rem(tile, blocks_per_seq)
    seq_start = (tile - i) * blk
    tri2 = tri2_ref[...]
    causal = (lax.broadcasted_iota(jnp.int32, (blk, blk), 1)
              < lax.broadcasted_iota(jnp.int32, (blk, blk), 0))
    prev_bias = jnp.where(i > 0, 0.0, NO_BLOCK_BIAS)
    scale_log2 = LOG2_E / math.sqrt(HEAD_DIM)

    def head(ref, part, h):
        return ref[:, part * d + h * HEAD_DIM:part * d + (h + 1) * HEAD_DIM]

    def log2_scores(q, k_blk):
        return lax.dot_general(q, k_blk, (((1,), (1,)), ((), ())), preferred_element_type=F32) * scale_log2

    def softplus2(z2):
        return jnp.maximum(z2, 0.0) + jnp.log(1.0 + jnp.exp2(-jnp.abs(z2))) * LOG2_E

    def tail_sum(x):
        hi = x.astype(BF16)
        lo = (x - hi.astype(F32)).astype(BF16)
        return _dot(jnp.concatenate([hi, lo], axis=1), tri2)

    def row_sum(x):
        return jnp.sum(x, axis=-1, keepdims=True)

    carries = []

    def attend(h):
        q = head(qkv_ref, 0, h)
        z2_diag = log2_scores(q, head(qkv_ref, 1, h))
        z2_prev = log2_scores(q, head(kv_prev, 0, h))
        yield
        sp2_diag = jnp.where(causal, softplus2(z2_diag), 0.0)
        sp2_prev = softplus2(z2_prev)
        tails = tail_sum(jnp.concatenate([sp2_diag, sp2_prev], axis=0))
        tail_diag, tail_prev = tails[:blk], tails[blk:]
        yield
        carry = -row_sum(sp2_diag)
        a_diag = jnp.where(causal, jnp.exp2(z2_diag - sp2_diag - tail_diag), 0.0)
        a_prev = jnp.exp2(z2_prev - sp2_prev - tail_prev + (carry + prev_bias))
        carry = carry - row_sum(sp2_prev)
        a = jnp.concatenate([a_diag, a_prev], axis=1).astype(BF16)
        v = jnp.concatenate([head(qkv_ref, 2, h), head(kv_prev, 1, h)], axis=0)
        acc = _dot(a, v)
        acc_ref[:, h * HEAD_DIM:(h + 1) * HEAD_DIM] = acc
        o_scr[:, h * HEAD_DIM:(h + 1) * HEAD_DIM] = acc.astype(o_scr.dtype)
        carry_ref[h] = carry
        carries.append(carry)

    heads = [attend(h) for h in range(N_HEADS)]
    live = [dense] + heads
    slot = 0
    while live:
        for gen in [dense] + [heads[h] for h in (slot - 4, slot - 2, slot) if 0 <= h < N_HEADS]:
            if gen in live and next(gen, "done") == "done":
                live.remove(gen)
        slot += 1
    max_carry = functools.reduce(jnp.maximum, carries)
    kv_prev[...] = qkv_ref[:, d:3 * d]

    def more(state):
        j, max_carry = state
        return jnp.logical_and(j >= 0, max_carry > TAIL_EXIT_LOG2)

    def step(state):
        j, _ = state
        start = pl.multiple_of(seq_start + j * blk, blk)
        copy = pltpu.make_async_copy(qkv_hbm.at[pl.ds(start, blk), pl.ds(d, 2 * d)], kv_buf, sem)
        copy.start()
        copy.wait()
        max_carry = None
        for h in range(N_HEADS):
            z2 = log2_scores(head(qkv_ref, 0, h), head(kv_buf, 0, h))
            sp2 = softplus2(z2)
            carry = carry_ref[h]
            a = jnp.exp2(z2 - sp2 - tail_sum(sp2) + carry)
            acc_ref[:, h * HEAD_DIM:(h + 1) * HEAD_DIM] += _dot(a.astype(BF16), head(kv_buf, 1, h))
            carry = carry - row_sum(sp2)
            carry_ref[h] = carry
            max_carry = carry if max_carry is None else jnp.maximum(max_carry, carry)
        o_scr[...] = acc_ref[...].astype(o_scr.dtype)
        return j - 1, jnp.max(max_carry)

    lax.while_loop(more, step, (i - 2, jnp.max(max_carry)))


def _attn_post(qkv, gates, h1, p, wao, wmo, n2, w_in, w_out, pn, wpg, wpp, fn, seq, final):
    t, d = h1.shape
    blk = ATTN_BLOCK
    n_tiles = t // blk
    tri = (lax.broadcasted_iota(jnp.int32, (blk, blk), 0)
           > lax.broadcasted_iota(jnp.int32, (blk, blk), 1)).astype(BF16)
    tri2 = jnp.concatenate([tri, tri], axis=0)

    def attn_rows(width):
        return pl.BlockSpec((blk, width), lambda s: (jnp.minimum(s, n_tiles - 1), 0))

    def tail_rows(width):
        return pl.BlockSpec((blk, width), lambda s: (jnp.maximum(s - 1, 0), 0))

    hbm = pl.BlockSpec(memory_space=pl.ANY)
    weights = (wao, wmo, n2, w_in, w_out, pn, wpg, wpp, fn)
    return pl.pallas_call(
        functools.partial(_attn_post_kernel, blocks_per_seq=seq // blk, n_tiles=n_tiles, final=final),
        out_shape=jax.ShapeDtypeStruct((t, d), F32),
        grid=(n_tiles + 1,),
        in_specs=[attn_rows(3 * d), _resident(tri2.shape), hbm, tail_rows(d), tail_rows(2 * d),
                  tail_rows(p.shape[1])] + [_resident(w.shape) for w in weights],
        out_specs=tail_rows(d),
        scratch_shapes=[pltpu.VMEM((blk, d), BF16), pltpu.VMEM((blk, 2 * d), BF16),
                        pltpu.VMEM((blk, d), F32), pltpu.VMEM((N_HEADS, blk, 1), F32),
                        pltpu.VMEM((blk, 2 * d), BF16), pltpu.SemaphoreType.DMA(())],
        compiler_params=_params(1),
        name="attn_post",
    )(qkv, tri2, qkv, h1, gates, p, *weights)


def kernel(x, p, ffn1_norm, ffn1_w_in, ffn1_w_out, mix_norm, w_mix_in, conv_w, w_conv_out, w_attn_out,
           w_mix_out, ffn2_norm, ffn2_w_in, ffn2_w_out, ple_norm, w_ple_gate, w_ple_proj, final_norm):
    b, s, d = x.shape
    depth = p.shape[0]
    assert d == N_HEADS * HEAD_DIM and conv_w.shape[1] == CONV_K
    assert s % max(FFN_ROWS, MIX_ROWS, ATTN_BLOCK) == 0

    def gain(g):
        return g.reshape(1, d).astype(F32)

    h = x.reshape(b * s, d)
    for i in range(depth):
        h1 = _ffn(h, gain(ffn1_norm[i]), ffn1_w_in[i].astype(BF16), ffn1_w_out[i].astype(BF16))
        qkv, gates = _mix(h1, gain(mix_norm[i]), w_mix_in[i].astype(BF16), conv_w[i],
                          w_conv_out[i].astype(BF16), s)
        h = _attn_post(qkv, gates, h1, p[i].reshape(b * s, -1),
                       w_attn_out[i].astype(BF16), w_mix_out[i].astype(BF16), gain(ffn2_norm[i]),
                       ffn2_w_in[i].astype(BF16), ffn2_w_out[i].astype(BF16), gain(ple_norm[i]),
                       w_ple_gate[i].astype(BF16), w_ple_proj[i].astype(BF16), gain(final_norm),
                       seq=s, final=(i == depth - 1))
    return h.reshape(b, s, d)
```

```python
import functools
import math

import jax
import jax.numpy as jnp
from jax import lax
from jax.experimental import pallas as pl
from jax.experimental.pallas import tpu as pltpu

NORM_EPS = 1e-6
N_HEADS = 8
HEAD_DIM = 128
CONV_K = 3

V7X_VMEM_LIMIT_BYTES = 56 * 1024 * 1024
SUBLANES = 8
BF16_SUBLANES = 16

FFN_ROWS = 512
MIX_ROWS = 512
ATTN_BLOCK = 256

LOG2_E = math.log2(math.e)
TAIL_EXIT_LOG2 = -150.0
NO_BLOCK_BIAS = -1e30

_dot = functools.partial(jnp.dot, preferred_element_type=jnp.float32)
BF16 = jnp.bfloat16
F32 = jnp.float32


def _rms_norm(x, g):
    ms = jnp.mean(x * x, axis=-1, keepdims=True)
    return x * lax.rsqrt(ms + NORM_EPS) * g


def _swiglu(xn, w_in_ref, w_out_ref):
    d_ff = w_out_ref.shape[0]
    gate = _dot(xn, w_in_ref[:, 0:d_ff])
    up = _dot(xn, w_in_ref[:, d_ff:2 * d_ff])
    act = (gate * jax.nn.sigmoid(gate) * up).astype(BF16)
    return _dot(act, w_out_ref[...])


def _resident(shape):
    return pl.BlockSpec(shape, lambda *_: (0,) * len(shape), pipeline_mode=pl.Buffered(1))


def _params(n_axes):
    return pltpu.CompilerParams(
        dimension_semantics=("arbitrary",) * n_axes,
        vmem_limit_bytes=V7X_VMEM_LIMIT_BYTES,
    )


def _cast_plan(rows, n_steps):
    for block in range(BF16_SUBLANES, rows + 1, BF16_SUBLANES):
        if rows % block == 0 and rows // block <= n_steps:
            return block, rows // block
    raise ValueError(f"no bf16 row block of {rows} rows fits {n_steps} steps")


def _ffn_kernel(x_ref, g_ref, w_in_ref, w_out_ref, *refs, n_steps, cast_steps):
    n = len(cast_steps)
    src_refs, o_ref, dst_refs = refs[:n], refs[n], refs[n + 1:]

    for src, dst, steps in zip(src_refs, dst_refs, cast_steps):
        if steps == n_steps:
            dst[...] = src[...].astype(BF16)
        else:
            @pl.when(pl.program_id(0) < steps)
            def _(src=src, dst=dst):
                dst[...] = src[...].astype(BF16)

    x = x_ref[...]
    xn = _rms_norm(x, g_ref[...]).astype(BF16)
    o_ref[...] = x + 0.5 * _swiglu(xn, w_in_ref, w_out_ref)


def _ffn(x, g, w_in, w_out, later_weights):
    t, d = x.shape
    n_steps = t // FFN_ROWS
    rows = pl.BlockSpec((FFN_ROWS, d), lambda i: (i, 0))
    plans = [_cast_plan(w.shape[0], n_steps) for w in later_weights]

    def cast_spec(w, plan):
        block, steps = plan
        return pl.BlockSpec((block, w.shape[1]), lambda i: (jnp.minimum(i, steps - 1), 0))

    cast_specs = [cast_spec(w, plan) for w, plan in zip(later_weights, plans)]
    h1, *copies = pl.pallas_call(
        functools.partial(_ffn_kernel, n_steps=n_steps, cast_steps=tuple(steps for _, steps in plans)),
        out_shape=[jax.ShapeDtypeStruct((t, d), F32)]
                  + [jax.ShapeDtypeStruct(w.shape, BF16) for w in later_weights],
        grid=(n_steps,),
        in_specs=[rows, _resident(g.shape), _resident(w_in.shape), _resident(w_out.shape)] + cast_specs,
        out_specs=[rows] + cast_specs,
        compiler_params=_params(1),
        name="ffn",
    )(x, g, w_in, w_out, *later_weights)
    return h1, copies


def _mix_kernel(h_ref, g_ref, w_ref, cw_ref, wco_ref, qkv_ref, gates_ref, cc_buf, *, tiles_per_seq):
    tm, d = h_ref.shape
    u = _rms_norm(h_ref[...], g_ref[...]).astype(BF16)

    def proj(j):
        return _dot(u, w_ref[:, j * d:(j + 1) * d])

    @pl.when(pl.program_id(0) % tiles_per_seq == 0)
    def _():
        cc_buf[0:SUBLANES, :] = jnp.zeros((SUBLANES, d), F32)

    cc = proj(1) * proj(2)
    cc_buf[SUBLANES:SUBLANES + tm, :] = cc
    c_b = proj(0)
    qkv_ref[:, 0:d] = proj(3).astype(BF16)
    conv = (cw_ref[0:1, :] * cc_buf[SUBLANES - 2:SUBLANES - 2 + tm, :]
            + cw_ref[1:2, :] * cc_buf[SUBLANES - 1:SUBLANES - 1 + tm, :]
            + cw_ref[2:3, :] * cc)
    cc_buf[0:SUBLANES, :] = cc_buf[tm:tm + SUBLANES, :]
    conv_gate = jax.nn.sigmoid(proj(6))
    gates_ref[:, 0:d] = jax.nn.sigmoid(proj(7)).astype(BF16)
    y_conv = _dot((c_b * conv).astype(BF16), wco_ref[...])
    qkv_ref[:, d:2 * d] = proj(4).astype(BF16)
    gates_ref[:, d:2 * d] = (conv_gate * y_conv).astype(BF16)
    qkv_ref[:, 2 * d:3 * d] = proj(5).astype(BF16)


def _mix(h, g, w_mix_in, conv_w, w_conv_out, seq):
    t, d = h.shape
    rows = pl.BlockSpec((MIX_ROWS, d), lambda i: (i, 0))
    return pl.pallas_call(
        functools.partial(_mix_kernel, tiles_per_seq=seq // MIX_ROWS),
        out_shape=(jax.ShapeDtypeStruct((t, 3 * d), BF16), jax.ShapeDtypeStruct((t, 2 * d), BF16)),
        grid=(t // MIX_ROWS,),
        in_specs=[rows, _resident(g.shape), _resident(w_mix_in.shape), _resident(conv_w.shape),
                  _resident(w_conv_out.shape)],
        out_specs=(pl.BlockSpec((MIX_ROWS, 3 * d), lambda i: (i, 0)),
                   pl.BlockSpec((MIX_ROWS, 2 * d), lambda i: (i, 0))),
        scratch_shapes=[pltpu.VMEM((MIX_ROWS + SUBLANES, d), F32)],
        compiler_params=_params(1),
        name="mix",
    )(h, g, w_mix_in, conv_w, w_conv_out)


FFN2_CHUNKS = ((0, 768), (768, 1536), (1536, 2304), (2304, 2816))


def _dense_tail_stages(o, h1_ref, gates_ref, p_ref, wao_ref, wmo_ref, n2_ref, w_in_ref, w_out_ref,
                       pn_ref, wpg_ref, wpp_ref, fn_ref, out_ref, final):
    d_ff = w_out_ref.shape[0]
    assert FFN2_CHUNKS[0][0] == 0 and FFN2_CHUNKS[-1][1] == d_ff
    d = o.shape[1]
    y_attn = _dot(o, wao_ref[...])
    merged = (gates_ref[:, d:2 * d].astype(F32) + gates_ref[:, 0:d].astype(F32) * y_attn).astype(BF16)
    yield
    h = h1_ref[...] + _dot(merged, wmo_ref[...])
    xn = _rms_norm(h, n2_ref[...]).astype(BF16)
    yield
    y = None
    for lo, hi in FFN2_CHUNKS:
        gate = _dot(xn, w_in_ref[:, lo:hi])
        up = _dot(xn, w_in_ref[:, d_ff + lo:d_ff + hi])
        act = (gate * jax.nn.sigmoid(gate) * up).astype(BF16)
        yield
        part = _dot(act, w_out_ref[lo:hi, :])
        y = part if y is None else y + part
        yield
    h = h + 0.5 * y
    ple_gate = jax.nn.sigmoid(_dot(_rms_norm(h, pn_ref[...]).astype(BF16), wpg_ref[...]))
    h = h + ple_gate * _dot(p_ref[...].astype(BF16), wpp_ref[...])
    out_ref[...] = _rms_norm(h, fn_ref[...]) if final else h


def _attn_post_kernel(qkv_ref, tri2_ref, qkv_hbm, h1_ref, gates_ref, p_ref, wao_ref, wmo_ref, n2_ref,
                      w_in_ref, w_out_ref, pn_ref, wpg_ref, wpp_ref, fn_ref, out_ref,
                      o_scr, kv_prev, acc_ref, carry_ref, kv_buf, sem, *, blocks_per_seq, n_tiles, final):
    blk, d = o_scr.shape
    s = pl.program_id(0)

    @pl.when(s == 0)
    def _():
        o_scr[...] = jnp.zeros_like(o_scr)
        kv_prev[...] = jnp.zeros_like(kv_prev)

    dense = _dense_tail_stages(o_scr[...], h1_ref, gates_ref, p_ref, wao_ref, wmo_ref, n2_ref,
                               w_in_ref, w_out_ref, pn_ref, wpg_ref, wpp_ref, fn_ref, out_ref, final)

    tile = jnp.minimum(s, n_tiles - 1)
    i = lax.rem(tile, blocks_per_seq)
    seq_start = (tile - i) * blk
    tri2 = tri2_ref[...]
    causal = (lax.broadcasted_iota(jnp.int32, (blk, blk), 1)
              < lax.broadcasted_iota(jnp.int32, (blk, blk), 0))
    prev_bias = jnp.where(i > 0, 0.0, NO_BLOCK_BIAS)
    scale_log2 = LOG2_E / math.sqrt(HEAD_DIM)

    def head(ref, part, h):
        return ref[:, part * d + h * HEAD_DIM:part * d + (h + 1) * HEAD_DIM]

    def log2_scores(q, k_blk):
        return lax.dot_general(q, k_blk, (((1,), (1,)), ((), ())), preferred_element_type=F32) * scale_log2

    def softplus2(z2):
        return jnp.maximum(z2, 0.0) + jnp.log(1.0 + jnp.exp2(-jnp.abs(z2))) * LOG2_E

    def tail_sum(x):
        hi = x.astype(BF16)
        lo = (x - hi.astype(F32)).astype(BF16)
        return _dot(jnp.concatenate([hi, lo], axis=1), tri2)

    def row_sum(x):
        return jnp.sum(x, axis=-1, keepdims=True)

    carries = []

    def attend(h):
        q = head(qkv_ref, 0, h)
        z2_diag = log2_scores(q, head(qkv_ref, 1, h))
        z2_prev = log2_scores(q, head(kv_prev, 0, h))
        yield
        sp2_diag = jnp.where(causal, softplus2(z2_diag), 0.0)
        sp2_prev = softplus2(z2_prev)
        tails = tail_sum(jnp.concatenate([sp2_diag, sp2_prev], axis=0))
        tail_diag, tail_prev = tails[:blk], tails[blk:]
        yield
        carry = -row_sum(sp2_diag)
        a_diag = jnp.where(causal, jnp.exp2(z2_diag - sp2_diag - tail_diag), 0.0)
        a_prev = jnp.exp2(z2_prev - sp2_prev - tail_prev + (carry + prev_bias))
        carry = carry - row_sum(sp2_prev)
        a = jnp.concatenate([a_diag, a_prev], axis=1).astype(BF16)
        v = jnp.concatenate([head(qkv_ref, 2, h), head(kv_prev, 1, h)], axis=0)
        acc = _dot(a, v)
        acc_ref[:, h * HEAD_DIM:(h + 1) * HEAD_DIM] = acc
        o_scr[:, h * HEAD_DIM:(h + 1) * HEAD_DIM] = acc.astype(o_scr.dtype)
        carry_ref[h] = carry
        carries.append(carry)

    heads = [attend(h) for h in range(N_HEADS)]
    live = [dense] + heads
    slot = 0
    while live:
        for gen in [dense] + [heads[h] for h in (slot - 4, slot - 2, slot) if 0 <= h < N_HEADS]:
            if gen in live and next(gen, "done") == "done":
                live.remove(gen)
        slot += 1
    max_carry = functools.reduce(jnp.maximum, carries)
    kv_prev[...] = qkv_ref[:, d:3 * d]

    def more(state):
        j, max_carry = state
        return jnp.logical_and(j >= 0, max_carry > TAIL_EXIT_LOG2)

    def step(state):
        j, _ = state
        start = pl.multiple_of(seq_start + j * blk, blk)
        copy = pltpu.make_async_copy(qkv_hbm.at[pl.ds(start, blk), pl.ds(d, 2 * d)], kv_buf, sem)
        copy.start()
        copy.wait()
        max_carry = None
        for h in range(N_HEADS):
            z2 = log2_scores(head(qkv_ref, 0, h), head(kv_buf, 0, h))
            sp2 = softplus2(z2)
            carry = carry_ref[h]
            a = jnp.exp2(z2 - sp2 - tail_sum(sp2) + carry)
            acc_ref[:, h * HEAD_DIM:(h + 1) * HEAD_DIM] += _dot(a.astype(BF16), head(kv_buf, 1, h))
            carry = carry - row_sum(sp2)
            carry_ref[h] = carry
            max_carry = carry if max_carry is None else jnp.maximum(max_carry, carry)
        o_scr[...] = acc_ref[...].astype(o_scr.dtype)
        return j - 1, jnp.max(max_carry)

    lax.while_loop(more, step, (i - 2, jnp.max(max_carry)))


def _attn_post(qkv, gates, h1, p, wao, wmo, n2, w_in, w_out, pn, wpg, wpp, fn, seq, final):
    t, d = h1.shape
    blk = ATTN_BLOCK
    n_tiles = t // blk
    tri = (lax.broadcasted_iota(jnp.int32, (blk, blk), 0)
           > lax.broadcasted_iota(jnp.int32, (blk, blk), 1)).astype(BF16)
    tri2 = jnp.concatenate([tri, tri], axis=0)

    def attn_rows(width):
        return pl.BlockSpec((blk, width), lambda s: (jnp.minimum(s, n_tiles - 1), 0))

    def tail_rows(width):
        return pl.BlockSpec((blk, width), lambda s: (jnp.maximum(s - 1, 0), 0))

    hbm = pl.BlockSpec(memory_space=pl.ANY)
    weights = (wao, wmo, n2, w_in, w_out, pn, wpg, wpp, fn)
    return pl.pallas_call(
        functools.partial(_attn_post_kernel, blocks_per_seq=seq // blk, n_tiles=n_tiles, final=final),
        out_shape=jax.ShapeDtypeStruct((t, d), F32),
        grid=(n_tiles + 1,),
        in_specs=[attn_rows(3 * d), _resident(tri2.shape), hbm, tail_rows(d), tail_rows(2 * d),
                  tail_rows(p.shape[1])] + [_resident(w.shape) for w in weights],
        out_specs=tail_rows(d),
        scratch_shapes=[pltpu.VMEM((blk, d), BF16), pltpu.VMEM((blk, 2 * d), BF16),
                        pltpu.VMEM((blk, d), F32), pltpu.VMEM((N_HEADS, blk, 1), F32),
                        pltpu.VMEM((blk, 2 * d), BF16), pltpu.SemaphoreType.DMA(())],
        compiler_params=_params(1),
        name="attn_post",
    )(qkv, tri2, qkv, h1, gates, p, *weights)


def kernel(x, p, ffn1_norm, ffn1_w_in, ffn1_w_out, mix_norm, w_mix_in, conv_w, w_conv_out, w_attn_out,
           w_mix_out, ffn2_norm, ffn2_w_in, ffn2_w_out, ple_norm, w_ple_gate, w_ple_proj, final_norm):
    b, s, d = x.shape
    depth = p.shape[0]
    assert d == N_HEADS * HEAD_DIM and conv_w.shape[1] == CONV_K
    assert s % max(FFN_ROWS, MIX_ROWS, ATTN_BLOCK) == 0

    def gain(g):
        return g.reshape(1, d).astype(F32)

    h = x.reshape(b * s, d)
    for i in range(depth):
        later = (w_mix_in[i], w_conv_out[i], w_attn_out[i], w_mix_out[i], ffn2_w_in[i], ffn2_w_out[i],
                 w_ple_gate[i])
        h1, (wmi, wco, wao, wmo, w2i, w2o, wpg) = _ffn(
            h, gain(ffn1_norm[i]), ffn1_w_in[i].astype(BF16), ffn1_w_out[i].astype(BF16), later)
        qkv, gates = _mix(h1, gain(mix_norm[i]), wmi, conv_w[i], wco, s)
        h = _attn_post(qkv, gates, h1, p[i].reshape(b * s, -1), wao, wmo, gain(ffn2_norm[i]), w2i, w2o,
                       gain(ple_norm[i]), wpg, w_ple_proj[i].astype(BF16), gain(final_norm),
                       seq=s, final=(i == depth - 1))
    return h.reshape(b, s, d)
```

```python
import functools
import math

import jax
import jax.numpy as jnp
from jax import lax
from jax.experimental import pallas as pl
from jax.experimental.pallas import tpu as pltpu

NORM_EPS = 1e-6
N_HEADS = 8
HEAD_DIM = 128
CONV_K = 3

V7X_VMEM_LIMIT_BYTES = 56 * 1024 * 1024
SUBLANES = 8
BF16_SUBLANES = 16

FFN_ROWS = 1024
FFN_CHUNKS = ((0, 768), (768, 1536), (1536, 2304), (2304, 2816))
MIX_ROWS = 512
ATTN_BLOCK = 256

LOG2_E = math.log2(math.e)
TAIL_EXIT_LOG2 = -150.0
NO_BLOCK_BIAS = -1e30

_dot = functools.partial(jnp.dot, preferred_element_type=jnp.float32)
BF16 = jnp.bfloat16
F32 = jnp.float32


def _rms_norm(x, g):
    ms = jnp.mean(x * x, axis=-1, keepdims=True)
    return x * lax.rsqrt(ms + NORM_EPS) * g


def _swiglu(xn, w_in_ref, w_out_ref):
    d_ff = w_out_ref.shape[0]
    y = None
    for lo, hi in FFN_CHUNKS:
        gate = _dot(xn, w_in_ref[:, lo:hi])
        up = _dot(xn, w_in_ref[:, d_ff + lo:d_ff + hi])
        act = (gate * jax.nn.sigmoid(gate) * up).astype(BF16)
        part = _dot(act, w_out_ref[lo:hi, :])
        y = part if y is None else y + part
    return y


def _resident(shape):
    return pl.BlockSpec(shape, lambda *_: (0,) * len(shape), pipeline_mode=pl.Buffered(1))


def _params(n_axes):
    return pltpu.CompilerParams(
        dimension_semantics=("arbitrary",) * n_axes,
        vmem_limit_bytes=V7X_VMEM_LIMIT_BYTES,
    )


def _cast_plan(rows, n_steps):
    for block in range(BF16_SUBLANES, rows + 1, BF16_SUBLANES):
        if rows % block == 0 and rows // block <= n_steps:
            return block, rows // block
    raise ValueError(f"no bf16 row block of {rows} rows fits {n_steps} steps")


def _ffn_kernel(x_ref, g_ref, w_in_ref, w_out_ref, *refs, n_steps, cast_steps):
    n = len(cast_steps)
    src_refs, o_ref, dst_refs = refs[:n], refs[n], refs[n + 1:]

    for src, dst, steps in zip(src_refs, dst_refs, cast_steps):
        if steps == n_steps:
            dst[...] = src[...].astype(BF16)
        else:
            @pl.when(pl.program_id(0) < steps)
            def _(src=src, dst=dst):
                dst[...] = src[...].astype(BF16)

    x = x_ref[...]
    xn = _rms_norm(x, g_ref[...]).astype(BF16)
    o_ref[...] = x + 0.5 * _swiglu(xn, w_in_ref, w_out_ref)


def _ffn(x, g, w_in, w_out, later_weights):
    t, d = x.shape
    n_steps = t // FFN_ROWS
    rows = pl.BlockSpec((FFN_ROWS, d), lambda i: (i, 0))
    plans = [_cast_plan(w.shape[0], n_steps) for w in later_weights]

    def cast_spec(w, plan):
        block, steps = plan
        return pl.BlockSpec((block, w.shape[1]), lambda i: (jnp.minimum(i, steps - 1), 0))

    cast_specs = [cast_spec(w, plan) for w, plan in zip(later_weights, plans)]
    h1, *copies = pl.pallas_call(
        functools.partial(_ffn_kernel, n_steps=n_steps, cast_steps=tuple(steps for _, steps in plans)),
        out_shape=[jax.ShapeDtypeStruct((t, d), F32)]
                  + [jax.ShapeDtypeStruct(w.shape, BF16) for w in later_weights],
        grid=(n_steps,),
        in_specs=[rows, _resident(g.shape), _resident(w_in.shape), _resident(w_out.shape)] + cast_specs,
        out_specs=[rows] + cast_specs,
        compiler_params=_params(1),
        name="ffn",
    )(x, g, w_in, w_out, *later_weights)
    return h1, copies


def _mix_kernel(h_ref, g_ref, w_ref, cw_ref, wco_ref, qkv_ref, gates_ref, cc_buf, *, tiles_per_seq):
    tm, d = h_ref.shape
    u = _rms_norm(h_ref[...], g_ref[...]).astype(BF16)

    def proj(j):
        return _dot(u, w_ref[:, j * d:(j + 1) * d])

    @pl.when(pl.program_id(0) % tiles_per_seq == 0)
    def _():
        cc_buf[0:SUBLANES, :] = jnp.zeros((SUBLANES, d), F32)

    cc = proj(1) * proj(2)
    cc_buf[SUBLANES:SUBLANES + tm, :] = cc
    c_b = proj(0)
    qkv_ref[:, 0:d] = proj(3).astype(BF16)
    conv = (cw_ref[0:1, :] * cc_buf[SUBLANES - 2:SUBLANES - 2 + tm, :]
            + cw_ref[1:2, :] * cc_buf[SUBLANES - 1:SUBLANES - 1 + tm, :]
            + cw_ref[2:3, :] * cc)
    cc_buf[0:SUBLANES, :] = cc_buf[tm:tm + SUBLANES, :]
    conv_gate = jax.nn.sigmoid(proj(6))
    gates_ref[:, 0:d] = jax.nn.sigmoid(proj(7)).astype(BF16)
    y_conv = _dot((c_b * conv).astype(BF16), wco_ref[...])
    qkv_ref[:, d:2 * d] = proj(4).astype(BF16)
    gates_ref[:, d:2 * d] = (conv_gate * y_conv).astype(BF16)
    qkv_ref[:, 2 * d:3 * d] = proj(5).astype(BF16)


def _mix(h, g, w_mix_in, conv_w, w_conv_out, seq):
    t, d = h.shape
    rows = pl.BlockSpec((MIX_ROWS, d), lambda i: (i, 0))
    return pl.pallas_call(
        functools.partial(_mix_kernel, tiles_per_seq=seq // MIX_ROWS),
        out_shape=(jax.ShapeDtypeStruct((t, 3 * d), BF16), jax.ShapeDtypeStruct((t, 2 * d), BF16)),
        grid=(t // MIX_ROWS,),
        in_specs=[rows, _resident(g.shape), _resident(w_mix_in.shape), _resident(conv_w.shape),
                  _resident(w_conv_out.shape)],
        out_specs=(pl.BlockSpec((MIX_ROWS, 3 * d), lambda i: (i, 0)),
                   pl.BlockSpec((MIX_ROWS, 2 * d), lambda i: (i, 0))),
        scratch_shapes=[pltpu.VMEM((MIX_ROWS + SUBLANES, d), F32)],
        compiler_params=_params(1),
        name="mix",
    )(h, g, w_mix_in, conv_w, w_conv_out)


HEAD_PHASE_GAP = 3


def _dense_tail_stages(o, h1_ref, gates_ref, p_ref, wao_ref, wmo_ref, n2_ref, w_in_ref, w_out_ref,
                       pn_ref, wpg_ref, wpp_ref, fn_ref, out_ref, final):
    d_ff = w_out_ref.shape[0]
    assert FFN_CHUNKS[0][0] == 0 and FFN_CHUNKS[-1][1] == d_ff
    d = o.shape[1]
    half = d // 2
    merged = []
    for lo in (0, half):
        y_attn = _dot(o, wao_ref[:, lo:lo + half])
        merged.append((gates_ref[:, d + lo:d + lo + half].astype(F32)
                       + gates_ref[:, lo:lo + half].astype(F32) * y_attn).astype(BF16))
        yield
    merged = jnp.concatenate(merged, axis=1)
    h = []
    for lo in (0, half):
        h.append(h1_ref[:, lo:lo + half] + _dot(merged, wmo_ref[:, lo:lo + half]))
        if lo == 0:
            yield
    h = jnp.concatenate(h, axis=1)
    xn = _rms_norm(h, n2_ref[...]).astype(BF16)
    yield
    y = None
    for lo, hi in FFN_CHUNKS:
        gate = _dot(xn, w_in_ref[:, lo:hi])
        up = _dot(xn, w_in_ref[:, d_ff + lo:d_ff + hi])
        act = (gate * jax.nn.sigmoid(gate) * up).astype(BF16)
        yield
        part = _dot(act, w_out_ref[lo:hi, :])
        y = part if y is None else y + part
        yield
    h = h + 0.5 * y
    ple_gate = jax.nn.sigmoid(_dot(_rms_norm(h, pn_ref[...]).astype(BF16), wpg_ref[...]))
    h = h + ple_gate * _dot(p_ref[...].astype(BF16), wpp_ref[...])
    out_ref[...] = _rms_norm(h, fn_ref[...]) if final else h


def _attn_post_kernel(qkv_ref, tri2_ref, qkv_hbm, h1_ref, gates_ref, p_ref, wao_ref, wmo_ref, n2_ref,
                      w_in_ref, w_out_ref, pn_ref, wpg_ref, wpp_ref, fn_ref, out_ref,
                      o_scr, kv_prev, acc_ref, carry_ref, kv_buf, sem, *, blocks_per_seq, n_tiles, final):
    blk, d = o_scr.shape
    s = pl.program_id(0)

    @pl.when(s == 0)
    def _():
        o_scr[...] = jnp.zeros_like(o_scr)
        kv_prev[...] = jnp.zeros_like(kv_prev)

    dense = _dense_tail_stages(o_scr[...], h1_ref, gates_ref, p_ref, wao_ref, wmo_ref, n2_ref,
                               w_in_ref, w_out_ref, pn_ref, wpg_ref, wpp_ref, fn_ref, out_ref, final)

    tile = jnp.minimum(s, n_tiles - 1)
    i = lax.rem(tile, blocks_per_seq)
    seq_start = (tile - i) * blk
    tri2 = tri2_ref[...]
    causal = (lax.broadcasted_iota(jnp.int32, (blk, blk), 1)
              < lax.broadcasted_iota(jnp.int32, (blk, blk), 0))
    prev_bias = jnp.where(i > 0, 0.0, NO_BLOCK_BIAS)
    scale_log2 = LOG2_E / math.sqrt(HEAD_DIM)

    def head(ref, part, h):
        return ref[:, part * d + h * HEAD_DIM:part * d + (h + 1) * HEAD_DIM]

    def log2_scores(q, k_blk):
        return lax.dot_general(q, k_blk, (((1,), (1,)), ((), ())), preferred_element_type=F32) * scale_log2

    def softplus2(z2):
        return jnp.maximum(z2, 0.0) + jnp.log(1.0 + jnp.exp2(-jnp.abs(z2))) * LOG2_E

    def tail_sum(x):
        hi = x.astype(BF16)
        lo = (x - hi.astype(F32)).astype(BF16)
        return _dot(jnp.concatenate([hi, lo], axis=1), tri2)

    def row_sum(x):
        return jnp.sum(x, axis=-1, keepdims=True)

    carries = []

    def attend(h):
        q = head(qkv_ref, 0, h)
        z2_diag = log2_scores(q, head(qkv_ref, 1, h))
        z2_prev = log2_scores(q, head(kv_prev, 0, h))
        yield
        sp2_diag = jnp.where(causal, softplus2(z2_diag), 0.0)
        sp2_prev = softplus2(z2_prev)
        tails = tail_sum(jnp.concatenate([sp2_diag, sp2_prev], axis=0))
        tail_diag, tail_prev = tails[:blk], tails[blk:]
        yield
        carry = -row_sum(sp2_diag)
        a_diag = jnp.where(causal, jnp.exp2(z2_diag - sp2_diag - tail_diag), 0.0)
        a_prev = jnp.exp2(z2_prev - sp2_prev - tail_prev + (carry + prev_bias))
        carry = carry - row_sum(sp2_prev)
        a = jnp.concatenate([a_diag, a_prev], axis=1).astype(BF16)
        v = jnp.concatenate([head(qkv_ref, 2, h), head(kv_prev, 1, h)], axis=0)
        acc = _dot(a, v)
        acc_ref[:, h * HEAD_DIM:(h + 1) * HEAD_DIM] = acc
        o_scr[:, h * HEAD_DIM:(h + 1) * HEAD_DIM] = acc.astype(o_scr.dtype)
        carry_ref[h] = carry
        carries.append(carry)

    heads = [attend(h) for h in range(N_HEADS)]
    live = [dense] + heads
    slot = 0
    while live:
        entering = (slot - 2 * HEAD_PHASE_GAP, slot - HEAD_PHASE_GAP, slot)
        for gen in [dense] + [heads[h] for h in entering if 0 <= h < N_HEADS]:
            if gen in live and next(gen, "done") == "done":
                live.remove(gen)
        slot += 1
    max_carry = functools.reduce(jnp.maximum, carries)
    kv_prev[...] = qkv_ref[:, d:3 * d]

    def more(state):
        j, max_carry = state
        return jnp.logical_and(j >= 0, max_carry > TAIL_EXIT_LOG2)

    def step(state):
        j, _ = state
        start = pl.multiple_of(seq_start + j * blk, blk)
        copy = pltpu.make_async_copy(qkv_hbm.at[pl.ds(start, blk), pl.ds(d, 2 * d)], kv_buf, sem)
        copy.start()
        copy.wait()
        max_carry = None
        for h in range(N_HEADS):
            z2 = log2_scores(head(qkv_ref, 0, h), head(kv_buf, 0, h))
            sp2 = softplus2(z2)
            carry = carry_ref[h]
            a = jnp.exp2(z2 - sp2 - tail_sum(sp2) + carry)
            acc_ref[:, h * HEAD_DIM:(h + 1) * HEAD_DIM] += _dot(a.astype(BF16), head(kv_buf, 1, h))
            carry = carry - row_sum(sp2)
            carry_ref[h] = carry
            max_carry = carry if max_carry is None else jnp.maximum(max_carry, carry)
        o_scr[...] = acc_ref[...].astype(o_scr.dtype)
        return j - 1, jnp.max(max_carry)

    lax.while_loop(more, step, (i - 2, jnp.max(max_carry)))


def _attn_post(qkv, gates, h1, p, wao, wmo, n2, w_in, w_out, pn, wpg, wpp, fn, seq, final):
    t, d = h1.shape
    blk = ATTN_BLOCK
    n_tiles = t // blk
    tri = (lax.broadcasted_iota(jnp.int32, (blk, blk), 0)
           > lax.broadcasted_iota(jnp.int32, (blk, blk), 1)).astype(BF16)
    tri2 = jnp.concatenate([tri, tri], axis=0)

    def attn_rows(width):
        return pl.BlockSpec((blk, width), lambda s: (jnp.minimum(s, n_tiles - 1), 0))

    def tail_rows(width):
        return pl.BlockSpec((blk, width), lambda s: (jnp.maximum(s - 1, 0), 0))

    hbm = pl.BlockSpec(memory_space=pl.ANY)
    weights = (wao, wmo, n2, w_in, w_out, pn, wpg, wpp, fn)
    return pl.pallas_call(
        functools.partial(_attn_post_kernel, blocks_per_seq=seq // blk, n_tiles=n_tiles, final=final),
        out_shape=jax.ShapeDtypeStruct((t, d), F32),
        grid=(n_tiles + 1,),
        in_specs=[attn_rows(3 * d), _resident(tri2.shape), hbm, tail_rows(d), tail_rows(2 * d),
                  tail_rows(p.shape[1])] + [_resident(w.shape) for w in weights],
        out_specs=tail_rows(d),
        scratch_shapes=[pltpu.VMEM((blk, d), BF16), pltpu.VMEM((blk, 2 * d), BF16),
                        pltpu.VMEM((blk, d), F32), pltpu.VMEM((N_HEADS, blk, 1), F32),
                        pltpu.VMEM((blk, 2 * d), BF16), pltpu.SemaphoreType.DMA(())],
        compiler_params=_params(1),
        name="attn_post",
    )(qkv, tri2, qkv, h1, gates, p, *weights)


def kernel(x, p, ffn1_norm, ffn1_w_in, ffn1_w_out, mix_norm, w_mix_in, conv_w, w_conv_out, w_attn_out,
           w_mix_out, ffn2_norm, ffn2_w_in, ffn2_w_out, ple_norm, w_ple_gate, w_ple_proj, final_norm):
    b, s, d = x.shape
    depth = p.shape[0]
    assert d == N_HEADS * HEAD_DIM and conv_w.shape[1] == CONV_K
    assert s % max(FFN_ROWS, MIX_ROWS, ATTN_BLOCK) == 0

    def gain(g):
        return g.reshape(1, d).astype(F32)

    h = x.reshape(b * s, d)
    for i in range(depth):
        later = (w_mix_in[i], w_conv_out[i], w_attn_out[i], w_mix_out[i], ffn2_w_in[i], ffn2_w_out[i],
                 w_ple_gate[i])
        h1, (wmi, wco, wao, wmo, w2i, w2o, wpg) = _ffn(
            h, gain(ffn1_norm[i]), ffn1_w_in[i].astype(BF16), ffn1_w_out[i].astype(BF16), later)
        qkv, gates = _mix(h1, gain(mix_norm[i]), wmi, conv_w[i], wco, s)
        h = _attn_post(qkv, gates, h1, p[i].reshape(b * s, -1), wao, wmo, gain(ffn2_norm[i]), w2i, w2o,
                       gain(ple_norm[i]), wpg, w_ple_proj[i].astype(BF16), gain(final_norm),
                       seq=s, final=(i == depth - 1))
    return h.reshape(b, s, d)
```

```python
import functools
import math

import jax
import jax.numpy as jnp
from jax import lax
from jax.experimental import pallas as pl
from jax.experimental.pallas import tpu as pltpu

NORM_EPS = 1e-6
N_HEADS = 8
HEAD_DIM = 128
CONV_K = 3

V7X_VMEM_LIMIT_BYTES = 56 * 1024 * 1024
SUBLANES = 8
BF16_SUBLANES = 16

FFN_ROWS = 1024
FFN_CHUNKS = ((0, 768), (768, 1536), (1536, 2304), (2304, 2816))
MIX_ROWS = 512
ATTN_BLOCK = 256

LOG2_E = math.log2(math.e)
TAIL_EXIT_LOG2 = -150.0
NO_BLOCK_BIAS = -1e30

_dot = functools.partial(jnp.dot, preferred_element_type=jnp.float32)
BF16 = jnp.bfloat16
F32 = jnp.float32


def _rms_norm(x, g):
    ms = jnp.mean(x * x, axis=-1, keepdims=True)
    return x * lax.rsqrt(ms + NORM_EPS) * g


def _swiglu(xn, w_in_ref, w_out_ref):
    d_ff = w_out_ref.shape[0]
    y = None
    for lo, hi in FFN_CHUNKS:
        gate = _dot(xn, w_in_ref[:, lo:hi])
        up = _dot(xn, w_in_ref[:, d_ff + lo:d_ff + hi])
        act = (gate * jax.nn.sigmoid(gate) * up).astype(BF16)
        part = _dot(act, w_out_ref[lo:hi, :])
        y = part if y is None else y + part
    return y


def _resident(shape):
    return pl.BlockSpec(shape, lambda *_: (0,) * len(shape), pipeline_mode=pl.Buffered(1))


def _params(n_axes):
    return pltpu.CompilerParams(
        dimension_semantics=("arbitrary",) * n_axes,
        vmem_limit_bytes=V7X_VMEM_LIMIT_BYTES,
    )


def _cast_plan(rows, n_steps):
    for block in range(BF16_SUBLANES, rows + 1, BF16_SUBLANES):
        if rows % block == 0 and rows // block <= n_steps:
            return block, rows // block
    raise ValueError(f"no bf16 row block of {rows} rows fits {n_steps} steps")


def _ffn_kernel(x_ref, g_ref, w_in_ref, w_out_ref, *refs, n_steps, cast_steps):
    n = len(cast_steps)
    src_refs, o_ref, dst_refs = refs[:n], refs[n], refs[n + 1:]

    for src, dst, steps in zip(src_refs, dst_refs, cast_steps):
        if steps == n_steps:
            dst[...] = src[...].astype(BF16)
        else:
            @pl.when(pl.program_id(0) < steps)
            def _(src=src, dst=dst):
                dst[...] = src[...].astype(BF16)

    x = x_ref[...]
    xn = _rms_norm(x, g_ref[...]).astype(BF16)
    o_ref[...] = x + 0.5 * _swiglu(xn, w_in_ref, w_out_ref)


def _ffn(x, g, w_in, w_out, later_weights):
    t, d = x.shape
    n_steps = t // FFN_ROWS
    rows = pl.BlockSpec((FFN_ROWS, d), lambda i: (i, 0))
    plans = [_cast_plan(w.shape[0], n_steps) for w in later_weights]

    def cast_spec(w, plan):
        block, steps = plan
        return pl.BlockSpec((block, w.shape[1]), lambda i: (jnp.minimum(i, steps - 1), 0))

    cast_specs = [cast_spec(w, plan) for w, plan in zip(later_weights, plans)]
    h1, *copies = pl.pallas_call(
        functools.partial(_ffn_kernel, n_steps=n_steps, cast_steps=tuple(steps for _, steps in plans)),
        out_shape=[jax.ShapeDtypeStruct((t, d), F32)]
                  + [jax.ShapeDtypeStruct(w.shape, BF16) for w in later_weights],
        grid=(n_steps,),
        in_specs=[rows, _resident(g.shape), _resident(w_in.shape), _resident(w_out.shape)] + cast_specs,
        out_specs=[rows] + cast_specs,
        compiler_params=_params(1),
        name="ffn",
    )(x, g, w_in, w_out, *later_weights)
    return h1, copies


def _mix_kernel(h_ref, g_ref, w_ref, cw_ref, wco_ref, qkv_ref, gates_ref, cc_buf, *, tiles_per_seq):
    tm, d = h_ref.shape
    u = _rms_norm(h_ref[...], g_ref[...]).astype(BF16)

    def proj(j):
        return _dot(u, w_ref[:, j * d:(j + 1) * d])

    @pl.when(pl.program_id(0) % tiles_per_seq == 0)
    def _():
        cc_buf[0:SUBLANES, :] = jnp.zeros((SUBLANES, d), F32)

    cc = proj(1) * proj(2)
    cc_buf[SUBLANES:SUBLANES + tm, :] = cc
    c_b = proj(0)
    qkv_ref[:, 0:d] = proj(3).astype(BF16)
    conv = (cw_ref[0:1, :] * cc_buf[SUBLANES - 2:SUBLANES - 2 + tm, :]
            + cw_ref[1:2, :] * cc_buf[SUBLANES - 1:SUBLANES - 1 + tm, :]
            + cw_ref[2:3, :] * cc)
    cc_buf[0:SUBLANES, :] = cc_buf[tm:tm + SUBLANES, :]
    conv_gate = jax.nn.sigmoid(proj(6))
    gates_ref[:, 0:d] = jax.nn.sigmoid(proj(7)).astype(BF16)
    y_conv = _dot((c_b * conv).astype(BF16), wco_ref[...])
    qkv_ref[:, d:2 * d] = proj(4).astype(BF16)
    gates_ref[:, d:2 * d] = (conv_gate * y_conv).astype(BF16)
    qkv_ref[:, 2 * d:3 * d] = proj(5).astype(BF16)


def _mix(h, g, w_mix_in, conv_w, w_conv_out, seq):
    t, d = h.shape
    rows = pl.BlockSpec((MIX_ROWS, d), lambda i: (i, 0))
    return pl.pallas_call(
        functools.partial(_mix_kernel, tiles_per_seq=seq // MIX_ROWS),
        out_shape=(jax.ShapeDtypeStruct((t, 3 * d), BF16), jax.ShapeDtypeStruct((t, 2 * d), BF16)),
        grid=(t // MIX_ROWS,),
        in_specs=[rows, _resident(g.shape), _resident(w_mix_in.shape), _resident(conv_w.shape),
                  _resident(w_conv_out.shape)],
        out_specs=(pl.BlockSpec((MIX_ROWS, 3 * d), lambda i: (i, 0)),
                   pl.BlockSpec((MIX_ROWS, 2 * d), lambda i: (i, 0))),
        scratch_shapes=[pltpu.VMEM((MIX_ROWS + SUBLANES, d), F32)],
        compiler_params=_params(1),
        name="mix",
    )(h, g, w_mix_in, conv_w, w_conv_out)


HEAD_PHASE_GAP = 3


def _dense_tail_stages(o, h1_ref, gates_ref, p_ref, wao_ref, wmo_ref, n2_ref, w_in_ref, w_out_ref,
                       pn_ref, wpg_ref, wpp_ref, fn_ref, out_ref, final):
    d_ff = w_out_ref.shape[0]
    assert FFN_CHUNKS[0][0] == 0 and FFN_CHUNKS[-1][1] == d_ff
    d = o.shape[1]
    half = d // 2
    merged = []
    for lo in (0, half):
        y_attn = _dot(o, wao_ref[:, lo:lo + half])
        merged.append((gates_ref[:, d + lo:d + lo + half].astype(F32)
                       + gates_ref[:, lo:lo + half].astype(F32) * y_attn).astype(BF16))
        yield
    merged = jnp.concatenate(merged, axis=1)
    h = []
    for lo in (0, half):
        h.append(h1_ref[:, lo:lo + half] + _dot(merged, wmo_ref[:, lo:lo + half]))
        if lo == 0:
            yield
    h = jnp.concatenate(h, axis=1)
    xn = _rms_norm(h, n2_ref[...]).astype(BF16)
    yield
    y = None
    for lo, hi in FFN_CHUNKS:
        gate = _dot(xn, w_in_ref[:, lo:hi])
        up = _dot(xn, w_in_ref[:, d_ff + lo:d_ff + hi])
        act = (gate * jax.nn.sigmoid(gate) * up).astype(BF16)
        yield
        part = _dot(act, w_out_ref[lo:hi, :])
        y = part if y is None else y + part
        yield
    h = h + 0.5 * y
    ple_gate = jax.nn.sigmoid(_dot(_rms_norm(h, pn_ref[...]).astype(BF16), wpg_ref[...]))
    h = h + ple_gate * _dot(p_ref[...].astype(BF16), wpp_ref[...])
    out_ref[...] = _rms_norm(h, fn_ref[...]) if final else h


def _attn_post_kernel(qkv_ref, tri_ref, qkv_hbm, h1_ref, gates_ref, p_ref, wao_ref, wmo_ref, n2_ref,
                      w_in_ref, w_out_ref, pn_ref, wpg_ref, wpp_ref, fn_ref, out_ref,
                      o_scr, kv_prev, acc_ref, carry_ref, kv_buf, sem, *, blocks_per_seq, n_tiles, final):
    blk, d = o_scr.shape
    s = pl.program_id(0)

    @pl.when(s == 0)
    def _():
        o_scr[...] = jnp.zeros_like(o_scr)
        kv_prev[...] = jnp.zeros_like(kv_prev)

    dense = _dense_tail_stages(o_scr[...], h1_ref, gates_ref, p_ref, wao_ref, wmo_ref, n2_ref,
                               w_in_ref, w_out_ref, pn_ref, wpg_ref, wpp_ref, fn_ref, out_ref, final)

    tile = jnp.minimum(s, n_tiles - 1)
    i = lax.rem(tile, blocks_per_seq)
    seq_start = (tile - i) * blk
    tri = tri_ref[...]
    causal = (lax.broadcasted_iota(jnp.int32, (blk, blk), 1)
              < lax.broadcasted_iota(jnp.int32, (blk, blk), 0))
    prev_bias = jnp.where(i > 0, 0.0, NO_BLOCK_BIAS)
    scale_log2 = LOG2_E / math.sqrt(HEAD_DIM)

    def head(ref, part, h):
        return ref[:, part * d + h * HEAD_DIM:part * d + (h + 1) * HEAD_DIM]

    def log2_scores(q, k_blk):
        return lax.dot_general(q, k_blk, (((1,), (1,)), ((), ())), preferred_element_type=F32) * scale_log2

    def softplus2(z2):
        return jnp.maximum(z2, 0.0) + jnp.log(1.0 + jnp.exp2(-jnp.abs(z2))) * LOG2_E

    def tail_sum(x):
        return _dot(x.astype(BF16), tri)

    def row_sum(x):
        return jnp.sum(x, axis=-1, keepdims=True)

    carries = []

    def attend(h):
        q = head(qkv_ref, 0, h)
        z2_diag = log2_scores(q, head(qkv_ref, 1, h))
        z2_prev = log2_scores(q, head(kv_prev, 0, h))
        yield
        sp2_diag = jnp.where(causal, softplus2(z2_diag), 0.0)
        sp2_prev = softplus2(z2_prev)
        tails = tail_sum(jnp.concatenate([sp2_diag, sp2_prev], axis=0))
        tail_diag, tail_prev = tails[:blk], tails[blk:]
        yield
        carry = -row_sum(sp2_diag)
        a_diag = jnp.where(causal, jnp.exp2(z2_diag - sp2_diag - tail_diag), 0.0)
        a_prev = jnp.exp2(z2_prev - sp2_prev - tail_prev + (carry + prev_bias))
        carry = carry - row_sum(sp2_prev)
        a = jnp.concatenate([a_diag, a_prev], axis=1).astype(BF16)
        v = jnp.concatenate([head(qkv_ref, 2, h), head(kv_prev, 1, h)], axis=0)
        acc = _dot(a, v)
        acc_ref[:, h * HEAD_DIM:(h + 1) * HEAD_DIM] = acc
        o_scr[:, h * HEAD_DIM:(h + 1) * HEAD_DIM] = acc.astype(o_scr.dtype)
        carry_ref[h] = carry
        carries.append(carry)

    heads = [attend(h) for h in range(N_HEADS)]
    live = [dense] + heads
    slot = 0
    while live:
        entering = (slot - 2 * HEAD_PHASE_GAP, slot - HEAD_PHASE_GAP, slot)
        for gen in [dense] + [heads[h] for h in entering if 0 <= h < N_HEADS]:
            if gen in live and next(gen, "done") == "done":
                live.remove(gen)
        slot += 1
    max_carry = functools.reduce(jnp.maximum, carries)
    kv_prev[...] = qkv_ref[:, d:3 * d]

    def more(state):
        j, max_carry = state
        return jnp.logical_and(j >= 0, max_carry > TAIL_EXIT_LOG2)

    def step(state):
        j, _ = state
        start = pl.multiple_of(seq_start + j * blk, blk)
        copy = pltpu.make_async_copy(qkv_hbm.at[pl.ds(start, blk), pl.ds(d, 2 * d)], kv_buf, sem)
        copy.start()
        copy.wait()
        max_carry = None
        for h in range(N_HEADS):
            z2 = log2_scores(head(qkv_ref, 0, h), head(kv_buf, 0, h))
            sp2 = softplus2(z2)
            carry = carry_ref[h]
            a = jnp.exp2(z2 - sp2 - tail_sum(sp2) + carry)
            acc_ref[:, h * HEAD_DIM:(h + 1) * HEAD_DIM] += _dot(a.astype(BF16), head(kv_buf, 1, h))
            carry = carry - row_sum(sp2)
            carry_ref[h] = carry
            max_carry = carry if max_carry is None else jnp.maximum(max_carry, carry)
        o_scr[...] = acc_ref[...].astype(o_scr.dtype)
        return j - 1, jnp.max(max_carry)

    lax.while_loop(more, step, (i - 2, jnp.max(max_carry)))


def _attn_post(qkv, gates, h1, p, wao, wmo, n2, w_in, w_out, pn, wpg, wpp, fn, seq, final):
    t, d = h1.shape
    blk = ATTN_BLOCK
    n_tiles = t // blk
    tri = (lax.broadcasted_iota(jnp.int32, (blk, blk), 0)
           > lax.broadcasted_iota(jnp.int32, (blk, blk), 1)).astype(BF16)

    def attn_rows(width):
        return pl.BlockSpec((blk, width), lambda s: (jnp.minimum(s, n_tiles - 1), 0))

    def tail_rows(width):
        return pl.BlockSpec((blk, width), lambda s: (jnp.maximum(s - 1, 0), 0))

    hbm = pl.BlockSpec(memory_space=pl.ANY)
    weights = (wao, wmo, n2, w_in, w_out, pn, wpg, wpp, fn)
    return pl.pallas_call(
        functools.partial(_attn_post_kernel, blocks_per_seq=seq // blk, n_tiles=n_tiles, final=final),
        out_shape=jax.ShapeDtypeStruct((t, d), F32),
        grid=(n_tiles + 1,),
        in_specs=[attn_rows(3 * d), _resident(tri.shape), hbm, tail_rows(d), tail_rows(2 * d),
                  tail_rows(p.shape[1])] + [_resident(w.shape) for w in weights],
        out_specs=tail_rows(d),
        scratch_shapes=[pltpu.VMEM((blk, d), BF16), pltpu.VMEM((blk, 2 * d), BF16),
                        pltpu.VMEM((blk, d), F32), pltpu.VMEM((N_HEADS, blk, 1), F32),
                        pltpu.VMEM((blk, 2 * d), BF16), pltpu.SemaphoreType.DMA(())],
        compiler_params=_params(1),
        name="attn_post",
    )(qkv, tri, qkv, h1, gates, p, *weights)


def kernel(x, p, ffn1_norm, ffn1_w_in, ffn1_w_out, mix_norm, w_mix_in, conv_w, w_conv_out, w_attn_out,
           w_mix_out, ffn2_norm, ffn2_w_in, ffn2_w_out, ple_norm, w_ple_gate, w_ple_proj, final_norm):
    b, s, d = x.shape
    depth = p.shape[0]
    assert d == N_HEADS * HEAD_DIM and conv_w.shape[1] == CONV_K
    assert s % max(FFN_ROWS, MIX_ROWS, ATTN_BLOCK) == 0

    def gain(g):
        return g.reshape(1, d).astype(F32)

    h = x.reshape(b * s, d)
    for i in range(depth):
        later = (w_mix_in[i], w_conv_out[i], w_attn_out[i], w_mix_out[i], ffn2_w_in[i], ffn2_w_out[i],
                 w_ple_gate[i])
        h1, (wmi, wco, wao, wmo, w2i, w2o, wpg) = _ffn(
            h, gain(ffn1_norm[i]), ffn1_w_in[i].astype(BF16), ffn1_w_out[i].astype(BF16), later)
        qkv, gates = _mix(h1, gain(mix_norm[i]), wmi, conv_w[i], wco, s)
        h = _attn_post(qkv, gates, h1, p[i].reshape(b * s, -1), wao, wmo, gain(ffn2_norm[i]), w2i, w2o,
                       gain(ple_norm[i]), wpg, w_ple_proj[i].astype(BF16), gain(final_norm),
                       seq=s, final=(i == depth - 1))
    return h.reshape(b, s, d)
```

```python
import functools
import math

import jax
import jax.numpy as jnp
from jax import lax
from jax.experimental import pallas as pl
from jax.experimental.pallas import tpu as pltpu

NORM_EPS = 1e-6
N_HEADS = 8
HEAD_DIM = 128
CONV_K = 3

V7X_VMEM_LIMIT_BYTES = 56 * 1024 * 1024
SUBLANES = 8
BF16_SUBLANES = 16

FFN_ROWS = 512
FFN_CHUNKS = ((0, 768), (768, 1536), (1536, 2304), (2304, 2816))
MIX_ROWS = 512
ATTN_BLOCK = 256

LOG2_E = math.log2(math.e)
TAIL_EXIT_LOG2 = -150.0
NO_BLOCK_BIAS = -1e30

_dot = functools.partial(jnp.dot, preferred_element_type=jnp.float32)
BF16 = jnp.bfloat16
F32 = jnp.float32


def _rms_norm(x, g):
    ms = jnp.mean(x * x, axis=-1, keepdims=True)
    return x * lax.rsqrt(ms + NORM_EPS) * g


def _swiglu(xn, w_in_ref, w_out_ref, first_gate=None):
    d_ff = w_out_ref.shape[0]
    y = None
    for n, (lo, hi) in enumerate(FFN_CHUNKS):
        gate = first_gate if n == 0 and first_gate is not None else _dot(xn, w_in_ref[:, lo:hi])
        up = _dot(xn, w_in_ref[:, d_ff + lo:d_ff + hi])
        act = (gate * jax.nn.sigmoid(gate) * up).astype(BF16)
        part = _dot(act, w_out_ref[lo:hi, :])
        y = part if y is None else y + part
    return y


def _resident(shape):
    return pl.BlockSpec(shape, lambda *_: (0,) * len(shape), pipeline_mode=pl.Buffered(1))


def _params(n_axes):
    return pltpu.CompilerParams(
        dimension_semantics=("arbitrary",) * n_axes,
        vmem_limit_bytes=V7X_VMEM_LIMIT_BYTES,
    )


def _cast_plan(rows, n_steps):
    for block in range(BF16_SUBLANES, rows + 1, BF16_SUBLANES):
        if rows % block == 0 and rows // block <= n_steps:
            return block, rows // block
    raise ValueError(f"no bf16 row block of {rows} rows fits {n_steps} steps")


def _ffn_kernel(x_ref, x_first_ref, x_next_ref, g_ref, w_in_ref, w_out_ref, *refs, n_steps, cast_steps):
    n = len(cast_steps)
    src_refs, o_ref, dst_refs = refs[:n], refs[n], refs[n + 1:n + 1 + n]
    xn_buf, gate_buf = refs[n + 1 + n:]
    lo, hi = FFN_CHUNKS[0]

    def normed(ref):
        return _rms_norm(ref[...], g_ref[...]).astype(BF16)

    @pl.when(pl.program_id(0) == 0)
    def _():
        xn_first = normed(x_first_ref)
        xn_buf[...] = xn_first
        gate_buf[...] = _dot(xn_first, w_in_ref[:, lo:hi])

    for src, dst, steps in zip(src_refs, dst_refs, cast_steps):
        if steps == n_steps:
            dst[...] = src[...].astype(BF16)
        else:
            @pl.when(pl.program_id(0) < steps)
            def _(src=src, dst=dst):
                dst[...] = src[...].astype(BF16)

    o_ref[...] = x_ref[...] + 0.5 * _swiglu(xn_buf[...], w_in_ref, w_out_ref, first_gate=gate_buf[...])
    xn_next = normed(x_next_ref)
    xn_buf[...] = xn_next
    gate_buf[...] = _dot(xn_next, w_in_ref[:, lo:hi])


def _ffn(x, g, w_in, w_out, later_weights):
    t, d = x.shape
    n_steps = t // FFN_ROWS
    rows = pl.BlockSpec((FFN_ROWS, d), lambda i: (i, 0))
    first_rows = pl.BlockSpec((FFN_ROWS, d), lambda i: (0, 0), pipeline_mode=pl.Buffered(1))
    next_rows = pl.BlockSpec((FFN_ROWS, d), lambda i: (jnp.minimum(i + 1, n_steps - 1), 0))
    plans = [_cast_plan(w.shape[0], n_steps) for w in later_weights]

    def cast_spec(w, plan):
        block, steps = plan
        return pl.BlockSpec((block, w.shape[1]), lambda i: (jnp.minimum(i, steps - 1), 0))

    cast_specs = [cast_spec(w, plan) for w, plan in zip(later_weights, plans)]
    h1, *copies = pl.pallas_call(
        functools.partial(_ffn_kernel, n_steps=n_steps, cast_steps=tuple(steps for _, steps in plans)),
        out_shape=[jax.ShapeDtypeStruct((t, d), F32)]
                  + [jax.ShapeDtypeStruct(w.shape, BF16) for w in later_weights],
        grid=(n_steps,),
        in_specs=[rows, first_rows, next_rows, _resident(g.shape), _resident(w_in.shape),
                  _resident(w_out.shape)] + cast_specs,
        out_specs=[rows] + cast_specs,
        scratch_shapes=[pltpu.VMEM((FFN_ROWS, d), BF16),
                        pltpu.VMEM((FFN_ROWS, FFN_CHUNKS[0][1] - FFN_CHUNKS[0][0]), F32)],
        compiler_params=_params(1),
        name="ffn",
    )(x, x, x, g, w_in, w_out, *later_weights)
    return h1, copies


def _mix_kernel(h_first_ref, h_next_ref, g_ref, w_ref, cw_ref, wco_ref, qkv_ref, gates_ref,
                cc_buf, u_buf, c_c_buf, *, tiles_per_seq):
    tm, d = h_next_ref.shape

    def normed(h_ref):
        return _rms_norm(h_ref[...], g_ref[...]).astype(BF16)

    def proj_of(u, j):
        return _dot(u, w_ref[:, j * d:(j + 1) * d])

    @pl.when(pl.program_id(0) == 0)
    def _():
        u_first = normed(h_first_ref)
        u_buf[...] = u_first
        c_c_buf[...] = proj_of(u_first, 1)

    u = u_buf[...]

    def proj(j):
        return proj_of(u, j)

    @pl.when(pl.program_id(0) % tiles_per_seq == 0)
    def _():
        cc_buf[0:SUBLANES, :] = jnp.zeros((SUBLANES, d), F32)

    cc = c_c_buf[...] * proj(2)
    cc_buf[SUBLANES:SUBLANES + tm, :] = cc
    c_b = proj(0)
    qkv_ref[:, 0:d] = proj(3).astype(BF16)
    conv = (cw_ref[0:1, :] * cc_buf[SUBLANES - 2:SUBLANES - 2 + tm, :]
            + cw_ref[1:2, :] * cc_buf[SUBLANES - 1:SUBLANES - 1 + tm, :]
            + cw_ref[2:3, :] * cc)
    cc_buf[0:SUBLANES, :] = cc_buf[tm:tm + SUBLANES, :]
    conv_gate = jax.nn.sigmoid(proj(6))
    gates_ref[:, 0:d] = jax.nn.sigmoid(proj(7)).astype(BF16)
    y_conv = _dot((c_b * conv).astype(BF16), wco_ref[...])
    qkv_ref[:, d:2 * d] = proj(4).astype(BF16)
    gates_ref[:, d:2 * d] = (conv_gate * y_conv).astype(BF16)
    qkv_ref[:, 2 * d:3 * d] = proj(5).astype(BF16)
    u_next = normed(h_next_ref)
    u_buf[...] = u_next
    c_c_buf[...] = proj_of(u_next, 1)


def _mix(h, g, w_mix_in, conv_w, w_conv_out, seq):
    t, d = h.shape
    n_tiles = t // MIX_ROWS
    first_rows = pl.BlockSpec((MIX_ROWS, d), lambda i: (0, 0), pipeline_mode=pl.Buffered(1))
    next_rows = pl.BlockSpec((MIX_ROWS, d), lambda i: (jnp.minimum(i + 1, n_tiles - 1), 0))
    return pl.pallas_call(
        functools.partial(_mix_kernel, tiles_per_seq=seq // MIX_ROWS),
        out_shape=(jax.ShapeDtypeStruct((t, 3 * d), BF16), jax.ShapeDtypeStruct((t, 2 * d), BF16)),
        grid=(n_tiles,),
        in_specs=[first_rows, next_rows, _resident(g.shape), _resident(w_mix_in.shape),
                  _resident(conv_w.shape), _resident(w_conv_out.shape)],
        out_specs=(pl.BlockSpec((MIX_ROWS, 3 * d), lambda i: (i, 0)),
                   pl.BlockSpec((MIX_ROWS, 2 * d), lambda i: (i, 0))),
        scratch_shapes=[pltpu.VMEM((MIX_ROWS + SUBLANES, d), F32), pltpu.VMEM((MIX_ROWS, d), BF16),
                        pltpu.VMEM((MIX_ROWS, d), F32)],
        compiler_params=_params(1),
        name="mix",
    )(h, h, g, w_mix_in, conv_w, w_conv_out)


HEAD_PHASE_GAP = 3


def _dense_tail_stages(o, h1_ref, gates_ref, p_ref, wao_ref, wmo_ref, n2_ref, w_in_ref, w_out_ref,
                       pn_ref, wpg_ref, wpp_ref, fn_ref, out_ref, final):
    d_ff = w_out_ref.shape[0]
    assert FFN_CHUNKS[0][0] == 0 and FFN_CHUNKS[-1][1] == d_ff
    d = o.shape[1]
    half = d // 2
    merged = []
    for lo in (0, half):
        y_attn = _dot(o, wao_ref[:, lo:lo + half])
        merged.append((gates_ref[:, d + lo:d + lo + half].astype(F32)
                       + gates_ref[:, lo:lo + half].astype(F32) * y_attn).astype(BF16))
        yield
    merged = jnp.concatenate(merged, axis=1)
    h = []
    for lo in (0, half):
        h.append(h1_ref[:, lo:lo + half] + _dot(merged, wmo_ref[:, lo:lo + half]))
        if lo == 0:
            yield
    h = jnp.concatenate(h, axis=1)
    xn = _rms_norm(h, n2_ref[...]).astype(BF16)
    yield
    y = None
    for lo, hi in FFN_CHUNKS:
        gate = _dot(xn, w_in_ref[:, lo:hi])
        up = _dot(xn, w_in_ref[:, d_ff + lo:d_ff + hi])
        act = (gate * jax.nn.sigmoid(gate) * up).astype(BF16)
        yield
        part = _dot(act, w_out_ref[lo:hi, :])
        y = part if y is None else y + part
        yield
    h = h + 0.5 * y
    ple_gate = jax.nn.sigmoid(_dot(_rms_norm(h, pn_ref[...]).astype(BF16), wpg_ref[...]))
    h = h + ple_gate * _dot(p_ref[...].astype(BF16), wpp_ref[...])
    out_ref[...] = _rms_norm(h, fn_ref[...]) if final else h


def _attn_post_kernel(qkv_ref, tri_ref, qkv_hbm, h1_ref, gates_ref, p_ref, wao_ref, wmo_ref, n2_ref,
                      w_in_ref, w_out_ref, pn_ref, wpg_ref, wpp_ref, fn_ref, out_ref,
                      o_scr, kv_prev, acc_ref, carry_ref, kv_buf, sem, *, blocks_per_seq, n_tiles, final):
    blk, d = o_scr.shape
    s = pl.program_id(0)

    @pl.when(s == 0)
    def _():
        o_scr[...] = jnp.zeros_like(o_scr)
        kv_prev[...] = jnp.zeros_like(kv_prev)

    dense = _dense_tail_stages(o_scr[...], h1_ref, gates_ref, p_ref, wao_ref, wmo_ref, n2_ref,
                               w_in_ref, w_out_ref, pn_ref, wpg_ref, wpp_ref, fn_ref, out_ref, final)

    tile = jnp.minimum(s, n_tiles - 1)
    i = lax.rem(tile, blocks_per_seq)
    seq_start = (tile - i) * blk
    tri = tri_ref[...]
    causal = (lax.broadcasted_iota(jnp.int32, (blk, blk), 1)
              < lax.broadcasted_iota(jnp.int32, (blk, blk), 0))
    prev_bias = jnp.where(i > 0, 0.0, NO_BLOCK_BIAS)
    scale_log2 = LOG2_E / math.sqrt(HEAD_DIM)

    def head(ref, part, h):
        return ref[:, part * d + h * HEAD_DIM:part * d + (h + 1) * HEAD_DIM]

    def log2_scores(q, k_blk):
        return lax.dot_general(q, k_blk, (((1,), (1,)), ((), ())), preferred_element_type=F32) * scale_log2

    def softplus2(z2):
        return jnp.maximum(z2, 0.0) + jnp.log(1.0 + jnp.exp2(-jnp.abs(z2))) * LOG2_E

    def tail_sum(x):
        return _dot(x.astype(BF16), tri)

    def row_sum(x):
        return jnp.sum(x, axis=-1, keepdims=True)

    carries = []

    def attend(h):
        q = head(qkv_ref, 0, h)
        z2_diag = log2_scores(q, head(qkv_ref, 1, h))
        z2_prev = log2_scores(q, head(kv_prev, 0, h))
        yield
        sp2_diag = jnp.where(causal, softplus2(z2_diag), 0.0)
        sp2_prev = softplus2(z2_prev)
        tails = tail_sum(jnp.concatenate([sp2_diag, sp2_prev], axis=0))
        tail_diag, tail_prev = tails[:blk], tails[blk:]
        yield
        carry = -row_sum(sp2_diag)
        a_diag = jnp.where(causal, jnp.exp2(z2_diag - sp2_diag - tail_diag), 0.0)
        a_prev = jnp.exp2(z2_prev - sp2_prev - tail_prev + (carry + prev_bias))
        carry = carry - row_sum(sp2_prev)
        a = jnp.concatenate([a_diag, a_prev], axis=1).astype(BF16)
        v = jnp.concatenate([head(qkv_ref, 2, h), head(kv_prev, 1, h)], axis=0)
        acc = _dot(a, v)
        acc_ref[:, h * HEAD_DIM:(h + 1) * HEAD_DIM] = acc
        o_scr[:, h * HEAD_DIM:(h + 1) * HEAD_DIM] = acc.astype(o_scr.dtype)
        carry_ref[h] = carry
        carries.append(carry)

    heads = [attend(h) for h in range(N_HEADS)]
    live = [dense] + heads
    slot = 0
    while live:
        entering = (slot - 2 * HEAD_PHASE_GAP, slot - HEAD_PHASE_GAP, slot)
        for gen in [dense] + [heads[h] for h in entering if 0 <= h < N_HEADS]:
            if gen in live and next(gen, "done") == "done":
                live.remove(gen)
        slot += 1
    max_carry = functools.reduce(jnp.maximum, carries)
    kv_prev[...] = qkv_ref[:, d:3 * d]

    def more(state):
        j, max_carry = state
        return jnp.logical_and(j >= 0, max_carry > TAIL_EXIT_LOG2)

    def step(state):
        j, _ = state
        start = pl.multiple_of(seq_start + j * blk, blk)
        copy = pltpu.make_async_copy(qkv_hbm.at[pl.ds(start, blk), pl.ds(d, 2 * d)], kv_buf, sem)
        copy.start()
        copy.wait()
        max_carry = None
        for h in range(N_HEADS):
            z2 = log2_scores(head(qkv_ref, 0, h), head(kv_buf, 0, h))
            sp2 = softplus2(z2)
            carry = carry_ref[h]
            a = jnp.exp2(z2 - sp2 - tail_sum(sp2) + carry)
            acc_ref[:, h * HEAD_DIM:(h + 1) * HEAD_DIM] += _dot(a.astype(BF16), head(kv_buf, 1, h))
            carry = carry - row_sum(sp2)
            carry_ref[h] = carry
            max_carry = carry if max_carry is None else jnp.maximum(max_carry, carry)
        o_scr[...] = acc_ref[...].astype(o_scr.dtype)
        return j - 1, jnp.max(max_carry)

    lax.while_loop(more, step, (i - 2, jnp.max(max_carry)))


def _attn_post(qkv, gates, h1, p, wao, wmo, n2, w_in, w_out, pn, wpg, wpp, fn, seq, final):
    t, d = h1.shape
    blk = ATTN_BLOCK
    n_tiles = t // blk
    tri = (lax.broadcasted_iota(jnp.int32, (blk, blk), 0)
           > lax.broadcasted_iota(jnp.int32, (blk, blk), 1)).astype(BF16)

    def attn_rows(width):
        return pl.BlockSpec((blk, width), lambda s: (jnp.minimum(s, n_tiles - 1), 0))

    def tail_rows(width):
        return pl.BlockSpec((blk, width), lambda s: (jnp.maximum(s - 1, 0), 0))

    hbm = pl.BlockSpec(memory_space=pl.ANY)
    weights = (wao, wmo, n2, w_in, w_out, pn, wpg, wpp, fn)
    return pl.pallas_call(
        functools.partial(_attn_post_kernel, blocks_per_seq=seq // blk, n_tiles=n_tiles, final=final),
        out_shape=jax.ShapeDtypeStruct((t, d), F32),
        grid=(n_tiles + 1,),
        in_specs=[attn_rows(3 * d), _resident(tri.shape), hbm, tail_rows(d), tail_rows(2 * d),
                  tail_rows(p.shape[1])] + [_resident(w.shape) for w in weights],
        out_specs=tail_rows(d),
        scratch_shapes=[pltpu.VMEM((blk, d), BF16), pltpu.VMEM((blk, 2 * d), BF16),
                        pltpu.VMEM((blk, d), F32), pltpu.VMEM((N_HEADS, blk, 1), F32),
                        pltpu.VMEM((blk, 2 * d), BF16), pltpu.SemaphoreType.DMA(())],
        compiler_params=_params(1),
        name="attn_post",
    )(qkv, tri, qkv, h1, gates, p, *weights)


def kernel(x, p, ffn1_norm, ffn1_w_in, ffn1_w_out, mix_norm, w_mix_in, conv_w, w_conv_out, w_attn_out,
           w_mix_out, ffn2_norm, ffn2_w_in, ffn2_w_out, ple_norm, w_ple_gate, w_ple_proj, final_norm):
    b, s, d = x.shape
    depth = p.shape[0]
    assert d == N_HEADS * HEAD_DIM and conv_w.shape[1] == CONV_K
    assert s % max(FFN_ROWS, MIX_ROWS, ATTN_BLOCK) == 0

    def gain(g):
        return g.reshape(1, d).astype(F32)

    h = x.reshape(b * s, d)
    for i in range(depth):
        later = (w_mix_in[i], w_conv_out[i], w_attn_out[i], w_mix_out[i], ffn2_w_in[i], ffn2_w_out[i],
                 w_ple_gate[i])
        h1, (wmi, wco, wao, wmo, w2i, w2o, wpg) = _ffn(
            h, gain(ffn1_norm[i]), ffn1_w_in[i].astype(BF16), ffn1_w_out[i].astype(BF16), later)
        qkv, gates = _mix(h1, gain(mix_norm[i]), wmi, conv_w[i], wco, s)
        h = _attn_post(qkv, gates, h1, p[i].reshape(b * s, -1), wao, wmo, gain(ffn2_norm[i]), w2i, w2o,
                       gain(ple_norm[i]), wpg, w_ple_proj[i].astype(BF16), gain(final_norm),
                       seq=s, final=(i == depth - 1))
    return h.reshape(b, s, d)
```

```python
import functools
import math

import jax
import jax.numpy as jnp
from jax import lax
from jax.experimental import pallas as pl
from jax.experimental.pallas import tpu as pltpu

NORM_EPS = 1e-6
N_HEADS = 8
HEAD_DIM = 128
CONV_K = 3

V7X_VMEM_LIMIT_BYTES = 56 * 1024 * 1024
SUBLANES = 8
BF16_SUBLANES = 16

FFN_ROWS = 1024
FFN_CHUNKS = ((0, 768), (768, 1536), (1536, 2304), (2304, 2816))
MIX_ROWS = 512
ATTN_BLOCK = 256

LOG2_E = math.log2(math.e)
TAIL_EXIT_LOG2 = -150.0
NO_BLOCK_BIAS = -1e30

_dot = functools.partial(jnp.dot, preferred_element_type=jnp.float32)
BF16 = jnp.bfloat16
F32 = jnp.float32


def _rms_norm(x, g):
    ms = jnp.mean(x * x, axis=-1, keepdims=True)
    return x * lax.rsqrt(ms + NORM_EPS) * g


def _swiglu(xn, w_in_ref, w_out_ref):
    d_ff = w_out_ref.shape[0]
    y = None
    for lo, hi in FFN_CHUNKS:
        gate = _dot(xn, w_in_ref[:, lo:hi])
        up = _dot(xn, w_in_ref[:, d_ff + lo:d_ff + hi])
        act = (gate * jax.nn.sigmoid(gate) * up).astype(BF16)
        part = _dot(act, w_out_ref[lo:hi, :])
        y = part if y is None else y + part
    return y


def _resident(shape):
    return pl.BlockSpec(shape, lambda *_: (0,) * len(shape), pipeline_mode=pl.Buffered(1))


def _params(n_axes):
    return pltpu.CompilerParams(
        dimension_semantics=("arbitrary",) * n_axes,
        vmem_limit_bytes=V7X_VMEM_LIMIT_BYTES,
    )


def _cast_plan(rows, n_steps):
    for block in range(BF16_SUBLANES, rows + 1, BF16_SUBLANES):
        if rows % block == 0 and rows // block <= n_steps:
            return block, rows // block
    raise ValueError(f"no bf16 row block of {rows} rows fits {n_steps} steps")


def _ffn_kernel(x_ref, g_ref, w_in_ref, w_out_ref, *refs, n_steps, cast_steps):
    n = len(cast_steps)
    src_refs, o_ref, dst_refs = refs[:n], refs[n], refs[n + 1:]

    for src, dst, steps in zip(src_refs, dst_refs, cast_steps):
        if steps == n_steps:
            dst[...] = src[...].astype(BF16)
        else:
            @pl.when(pl.program_id(0) < steps)
            def _(src=src, dst=dst):
                dst[...] = src[...].astype(BF16)

    x = x_ref[...]
    xn = _rms_norm(x, g_ref[...]).astype(BF16)
    o_ref[...] = x + 0.5 * _swiglu(xn, w_in_ref, w_out_ref)


def _ffn(x, g, w_in, w_out, later_weights):
    t, d = x.shape
    n_steps = t // FFN_ROWS
    rows = pl.BlockSpec((FFN_ROWS, d), lambda i: (i, 0))
    plans = [_cast_plan(w.shape[0], n_steps) for w in later_weights]

    def cast_spec(w, plan):
        block, steps = plan
        return pl.BlockSpec((block, w.shape[1]), lambda i: (jnp.minimum(i, steps - 1), 0))

    cast_specs = [cast_spec(w, plan) for w, plan in zip(later_weights, plans)]
    h1, *copies = pl.pallas_call(
        functools.partial(_ffn_kernel, n_steps=n_steps, cast_steps=tuple(steps for _, steps in plans)),
        out_shape=[jax.ShapeDtypeStruct((t, d), F32)]
                  + [jax.ShapeDtypeStruct(w.shape, BF16) for w in later_weights],
        grid=(n_steps,),
        in_specs=[rows, _resident(g.shape), _resident(w_in.shape), _resident(w_out.shape)] + cast_specs,
        out_specs=[rows] + cast_specs,
        compiler_params=_params(1),
        name="ffn",
    )(x, g, w_in, w_out, *later_weights)
    return h1, copies


def _mix_kernel(h_first_ref, h_next_ref, g_ref, w_ref, cw_ref, wco_ref, qkv_ref, gates_ref,
                cc_buf, u_buf, c_c_buf, *, tiles_per_seq):
    tm, d = h_next_ref.shape

    def normed(h_ref):
        return _rms_norm(h_ref[...], g_ref[...]).astype(BF16)

    def proj_of(u, j):
        return _dot(u, w_ref[:, j * d:(j + 1) * d])

    @pl.when(pl.program_id(0) == 0)
    def _():
        u_first = normed(h_first_ref)
        u_buf[...] = u_first
        c_c_buf[...] = proj_of(u_first, 1)

    u = u_buf[...]

    def proj(j):
        return proj_of(u, j)

    @pl.when(pl.program_id(0) % tiles_per_seq == 0)
    def _():
        cc_buf[0:SUBLANES, :] = jnp.zeros((SUBLANES, d), F32)

    cc = c_c_buf[...] * proj(2)
    cc_buf[SUBLANES:SUBLANES + tm, :] = cc
    c_b = proj(0)
    qkv_ref[:, 0:d] = proj(3).astype(BF16)
    conv = (cw_ref[0:1, :] * cc_buf[SUBLANES - 2:SUBLANES - 2 + tm, :]
            + cw_ref[1:2, :] * cc_buf[SUBLANES - 1:SUBLANES - 1 + tm, :]
            + cw_ref[2:3, :] * cc)
    cc_buf[0:SUBLANES, :] = cc_buf[tm:tm + SUBLANES, :]
    conv_gate = jax.nn.sigmoid(proj(6))
    gates_ref[:, 0:d] = jax.nn.sigmoid(proj(7)).astype(BF16)
    y_conv = _dot((c_b * conv).astype(BF16), wco_ref[...])
    qkv_ref[:, d:2 * d] = proj(4).astype(BF16)
    gates_ref[:, d:2 * d] = (conv_gate * y_conv).astype(BF16)
    qkv_ref[:, 2 * d:3 * d] = proj(5).astype(BF16)
    u_next = normed(h_next_ref)
    u_buf[...] = u_next
    c_c_buf[...] = proj_of(u_next, 1)


def _mix(h, g, w_mix_in, conv_w, w_conv_out, seq):
    t, d = h.shape
    n_tiles = t // MIX_ROWS
    first_rows = pl.BlockSpec((MIX_ROWS, d), lambda i: (0, 0), pipeline_mode=pl.Buffered(1))
    next_rows = pl.BlockSpec((MIX_ROWS, d), lambda i: (jnp.minimum(i + 1, n_tiles - 1), 0))
    return pl.pallas_call(
        functools.partial(_mix_kernel, tiles_per_seq=seq // MIX_ROWS),
        out_shape=(jax.ShapeDtypeStruct((t, 3 * d), BF16), jax.ShapeDtypeStruct((t, 2 * d), BF16)),
        grid=(n_tiles,),
        in_specs=[first_rows, next_rows, _resident(g.shape), _resident(w_mix_in.shape),
                  _resident(conv_w.shape), _resident(w_conv_out.shape)],
        out_specs=(pl.BlockSpec((MIX_ROWS, 3 * d), lambda i: (i, 0)),
                   pl.BlockSpec((MIX_ROWS, 2 * d), lambda i: (i, 0))),
        scratch_shapes=[pltpu.VMEM((MIX_ROWS + SUBLANES, d), F32), pltpu.VMEM((MIX_ROWS, d), BF16),
                        pltpu.VMEM((MIX_ROWS, d), F32)],
        compiler_params=_params(1),
        name="mix",
    )(h, h, g, w_mix_in, conv_w, w_conv_out)


HEAD_PHASE_GAP = 3


def _dense_tail_stages(o, h1_ref, gates_ref, p_ref, wao_ref, wmo_ref, n2_ref, w_in_ref, w_out_ref,
                       pn_ref, wpg_ref, wpp_ref, fn_ref, out_ref, final):
    d_ff = w_out_ref.shape[0]
    assert FFN_CHUNKS[0][0] == 0 and FFN_CHUNKS[-1][1] == d_ff
    d = o.shape[1]
    half = d // 2
    merged = []
    for lo in (0, half):
        y_attn = _dot(o, wao_ref[:, lo:lo + half])
        merged.append((gates_ref[:, d + lo:d + lo + half].astype(F32)
                       + gates_ref[:, lo:lo + half].astype(F32) * y_attn).astype(BF16))
        yield
    merged = jnp.concatenate(merged, axis=1)
    h = []
    for lo in (0, half):
        h.append(h1_ref[:, lo:lo + half] + _dot(merged, wmo_ref[:, lo:lo + half]))
        if lo == 0:
            yield
    h = jnp.concatenate(h, axis=1)
    xn = _rms_norm(h, n2_ref[...]).astype(BF16)
    yield
    y = None
    for lo, hi in FFN_CHUNKS:
        gate = _dot(xn, w_in_ref[:, lo:hi])
        up = _dot(xn, w_in_ref[:, d_ff + lo:d_ff + hi])
        act = (gate * jax.nn.sigmoid(gate) * up).astype(BF16)
        yield
        part = _dot(act, w_out_ref[lo:hi, :])
        y = part if y is None else y + part
        yield
    h = h + 0.5 * y
    ple_gate = jax.nn.sigmoid(_dot(_rms_norm(h, pn_ref[...]).astype(BF16), wpg_ref[...]))
    h = h + ple_gate * _dot(p_ref[...].astype(BF16), wpp_ref[...])
    out_ref[...] = _rms_norm(h, fn_ref[...]) if final else h


def _attn_post_kernel(qkv_ref, tri_ref, qkv_hbm, h1_ref, gates_ref, p_ref, wao_ref, wmo_ref, n2_ref,
                      w_in_ref, w_out_ref, pn_ref, wpg_ref, wpp_ref, fn_ref, out_ref,
                      o_scr, kv_prev, acc_ref, carry_ref, kv_buf, sem, *, blocks_per_seq, n_tiles, final):
    blk, d = o_scr.shape
    s = pl.program_id(0)

    @pl.when(s == 0)
    def _():
        o_scr[...] = jnp.zeros_like(o_scr)
        kv_prev[...] = jnp.zeros_like(kv_prev)

    dense = _dense_tail_stages(o_scr[...], h1_ref, gates_ref, p_ref, wao_ref, wmo_ref, n2_ref,
                               w_in_ref, w_out_ref, pn_ref, wpg_ref, wpp_ref, fn_ref, out_ref, final)

    tile = jnp.minimum(s, n_tiles - 1)
    i = lax.rem(tile, blocks_per_seq)
    seq_start = (tile - i) * blk
    tri = tri_ref[...]
    causal = (lax.broadcasted_iota(jnp.int32, (blk, blk), 1)
              < lax.broadcasted_iota(jnp.int32, (blk, blk), 0))
    prev_bias = jnp.where(i > 0, 0.0, NO_BLOCK_BIAS)
    scale_log2 = LOG2_E / math.sqrt(HEAD_DIM)

    def head(ref, part, h):
        return ref[:, part * d + h * HEAD_DIM:part * d + (h + 1) * HEAD_DIM]

    def log2_scores(q, k_blk):
        return lax.dot_general(q, k_blk, (((1,), (1,)), ((), ())), preferred_element_type=F32) * scale_log2

    def softplus2(z2):
        return jnp.maximum(z2, 0.0) + jnp.log(1.0 + jnp.exp2(-jnp.abs(z2))) * LOG2_E

    def tail_sum(x):
        return _dot(x.astype(BF16), tri)

    def row_sum(x):
        return jnp.sum(x, axis=-1, keepdims=True)

    carries = []

    def attend(h):
        q = head(qkv_ref, 0, h)
        z2_diag = log2_scores(q, head(qkv_ref, 1, h))
        z2_prev = log2_scores(q, head(kv_prev, 0, h))
        yield
        sp2_diag = jnp.where(causal, softplus2(z2_diag), 0.0)
        sp2_prev = softplus2(z2_prev)
        tails = tail_sum(jnp.concatenate([sp2_diag, sp2_prev], axis=0))
        tail_diag, tail_prev = tails[:blk], tails[blk:]
        yield
        carry = -row_sum(sp2_diag)
        a_diag = jnp.where(causal, jnp.exp2(z2_diag - sp2_diag - tail_diag), 0.0)
        a_prev = jnp.exp2(z2_prev - sp2_prev - tail_prev + (carry + prev_bias))
        carry = carry - row_sum(sp2_prev)
        a = jnp.concatenate([a_diag, a_prev], axis=1).astype(BF16)
        v = jnp.concatenate([head(qkv_ref, 2, h), head(kv_prev, 1, h)], axis=0)
        acc = _dot(a, v)
        acc_ref[:, h * HEAD_DIM:(h + 1) * HEAD_DIM] = acc
        o_scr[:, h * HEAD_DIM:(h + 1) * HEAD_DIM] = acc.astype(o_scr.dtype)
        carry_ref[h] = carry
        carries.append(carry)

    heads = [attend(h) for h in range(N_HEADS)]
    live = [dense] + heads
    slot = 0
    while live:
        entering = (slot - 2 * HEAD_PHASE_GAP, slot - HEAD_PHASE_GAP, slot)
        for gen in [dense] + [heads[h] for h in entering if 0 <= h < N_HEADS]:
            if gen in live and next(gen, "done") == "done":
                live.remove(gen)
        slot += 1
    max_carry = functools.reduce(jnp.maximum, carries)
    kv_prev[...] = qkv_ref[:, d:3 * d]

    def more(state):
        j, max_carry = state
        return jnp.logical_and(j >= 0, max_carry > TAIL_EXIT_LOG2)

    def step(state):
        j, _ = state
        start = pl.multiple_of(seq_start + j * blk, blk)
        copy = pltpu.make_async_copy(qkv_hbm.at[pl.ds(start, blk), pl.ds(d, 2 * d)], kv_buf, sem)
        copy.start()
        copy.wait()
        max_carry = None
        for h in range(N_HEADS):
            z2 = log2_scores(head(qkv_ref, 0, h), head(kv_buf, 0, h))
            sp2 = softplus2(z2)
            carry = carry_ref[h]
            a = jnp.exp2(z2 - sp2 - tail_sum(sp2) + carry)
            acc_ref[:, h * HEAD_DIM:(h + 1) * HEAD_DIM] += _dot(a.astype(BF16), head(kv_buf, 1, h))
            carry = carry - row_sum(sp2)
            carry_ref[h] = carry
            max_carry = carry if max_carry is None else jnp.maximum(max_carry, carry)
        o_scr[...] = acc_ref[...].astype(o_scr.dtype)
        return j - 1, jnp.max(max_carry)

    lax.while_loop(more, step, (i - 2, jnp.max(max_carry)))


def _attn_post(qkv, gates, h1, p, wao, wmo, n2, w_in, w_out, pn, wpg, wpp, fn, seq, final):
    t, d = h1.shape
    blk = ATTN_BLOCK
    n_tiles = t // blk
    tri = (lax.broadcasted_iota(jnp.int32, (blk, blk), 0)
           > lax.broadcasted_iota(jnp.int32, (blk, blk), 1)).astype(BF16)

    def attn_rows(width):
        return pl.BlockSpec((blk, width), lambda s: (jnp.minimum(s, n_tiles - 1), 0))

    def tail_rows(width):
        return pl.BlockSpec((blk, width), lambda s: (jnp.maximum(s - 1, 0), 0))

    hbm = pl.BlockSpec(memory_space=pl.ANY)
    weights = (wao, wmo, n2, w_in, w_out, pn, wpg, wpp, fn)
    return pl.pallas_call(
        functools.partial(_attn_post_kernel, blocks_per_seq=seq // blk, n_tiles=n_tiles, final=final),
        out_shape=jax.ShapeDtypeStruct((t, d), F32),
        grid=(n_tiles + 1,),
        in_specs=[attn_rows(3 * d), _resident(tri.shape), hbm, tail_rows(d), tail_rows(2 * d),
                  tail_rows(p.shape[1])] + [_resident(w.shape) for w in weights],
        out_specs=tail_rows(d),
        scratch_shapes=[pltpu.VMEM((blk, d), BF16), pltpu.VMEM((blk, 2 * d), BF16),
                        pltpu.VMEM((blk, d), F32), pltpu.VMEM((N_HEADS, blk, 1), F32),
                        pltpu.VMEM((blk, 2 * d), BF16), pltpu.SemaphoreType.DMA(())],
        compiler_params=_params(1),
        name="attn_post",
    )(qkv, tri, qkv, h1, gates, p, *weights)


def kernel(x, p, ffn1_norm, ffn1_w_in, ffn1_w_out, mix_norm, w_mix_in, conv_w, w_conv_out, w_attn_out,
           w_mix_out, ffn2_norm, ffn2_w_in, ffn2_w_out, ple_norm, w_ple_gate, w_ple_proj, final_norm):
    b, s, d = x.shape
    depth = p.shape[0]
    assert d == N_HEADS * HEAD_DIM and conv_w.shape[1] == CONV_K
    assert s % max(FFN_ROWS, MIX_ROWS, ATTN_BLOCK) == 0

    def gain(g):
        return g.reshape(1, d).astype(F32)

    h = x.reshape(b * s, d)
    for i in range(depth):
        later = (w_mix_in[i], w_conv_out[i], w_attn_out[i], w_mix_out[i], ffn2_w_in[i], ffn2_w_out[i],
                 w_ple_gate[i])
        h1, (wmi, wco, wao, wmo, w2i, w2o, wpg) = _ffn(
            h, gain(ffn1_norm[i]), ffn1_w_in[i].astype(BF16), ffn1_w_out[i].astype(BF16), later)
        qkv, gates = _mix(h1, gain(mix_norm[i]), wmi, conv_w[i], wco, s)
        h = _attn_post(qkv, gates, h1, p[i].reshape(b * s, -1), wao, wmo, gain(ffn2_norm[i]), w2i, w2o,
                       gain(ple_norm[i]), wpg, w_ple_proj[i].astype(BF16), gain(final_norm),
                       seq=s, final=(i == depth - 1))
    return h.reshape(b, s, d)
```
